```python
import jax, jax.numpy as jnp
from jax import lax
import numpy as np

D_MODEL = 2048
BATCH = 16
SEQ = 2048
DEPTH = 4
DEC_BATCH = 1
DEC_SEQ = 8192
PAST_LEN = 128

A_WIDTH = D_MODEL // 2
A_HEAD_DIM = 64
N_A_HEADS = A_WIDTH // A_HEAD_DIM
B_WIDTH = D_MODEL - A_WIDTH
B_HEAD_DIM = 64
N_B_HEADS = B_WIDTH // B_HEAD_DIM
CHUNK = 128
AB_IN = A_WIDTH + 2 * B_WIDTH
CONV_WIDTH = 3
C_WIDTH = D_MODEL
D_FF = ((8 * D_MODEL // 3 + 255) // 256) * 256
PLE_DIM = 256
N_EVEN = (DEPTH + 1) // 2
N_ODD = DEPTH // 2
EPS = 1e-6

kernel_name = 'hybrid_fourier_sgu_shortconv_encoder'


def rmsnorm(x, g):
    xf = x.astype(jnp.float32)
    y = xf * lax.rsqrt(jnp.mean(xf * xf, axis=-1, keepdims=True) + EPS)
    return (y * g.astype(jnp.float32)).astype(x.dtype)


def swiglu(h, w_in, w_out):
    gate, up = jnp.split(h @ w_in, 2, axis=-1)
    return (jax.nn.silu(gate) * up) @ w_out


def fourier_sgu_mixer(h, w_in, g_v, w_s, b_s, w_out):
    bsz, s, _ = h.shape
    z = h @ w_in
    za = z[..., :A_WIDTH].reshape(bsz, s, N_A_HEADS, A_HEAD_DIM)
    ya = jnp.fft.fft2(za.astype(jnp.float32), axes=(1, 3), norm='ortho').real.astype(h.dtype)
    zb = jax.nn.gelu(z[..., A_WIDTH:])
    u = zb[..., :B_WIDTH].reshape(bsz, s, N_B_HEADS, B_HEAD_DIM)
    v = rmsnorm(zb[..., B_WIDTH:].reshape(bsz, s, N_B_HEADS, B_HEAD_DIM), g_v)
    vc = v.reshape(bsz, s // CHUNK, CHUNK, N_B_HEADS, B_HEAD_DIM)
    mixed = jnp.einsum('hts,bnshc->bnthc', w_s, vc) + b_s.T[None, None, :, :, None]
    yb = u * mixed.reshape(bsz, s, N_B_HEADS, B_HEAD_DIM)
    y = jnp.concatenate([ya.reshape(bsz, s, A_WIDTH), yb.reshape(bsz, s, B_WIDTH)], axis=-1)
    return y @ w_out


def short_conv_mixer(h, w_in, w_conv, b_conv, w_out):
    gb, gc, xin = jnp.split(h @ w_in, 3, axis=-1)
    z = gc * xin
    zp = jnp.pad(z, ((0, 0), (1, 1), (0, 0)))
    conv = zp[:, :-2] * w_conv[0] + zp[:, 1:-1] * w_conv[1] + zp[:, 2:] * w_conv[2] + b_conv
    return (gb * conv) @ w_out


def trunk(x, p, g_ffn1, w_ffn1_in, w_ffn1_out, g_mix, w_in_ab, g_v, w_s, b_s, w_out_ab,
          w_in_c, w_conv, b_conv, w_out_c, g_ffn2, w_ffn2_in, w_ffn2_out,
          g_ple, w_ple_gate, w_ple, g_final):
    for i in range(DEPTH):
        x = x + 0.5 * swiglu(rmsnorm(x, g_ffn1[i]), w_ffn1_in[i], w_ffn1_out[i])
        h = rmsnorm(x, g_mix[i])
        j = i // 2
        if i % 2 == 0:
            x = x + fourier_sgu_mixer(h, w_in_ab[j], g_v[j], w_s[j], b_s[j], w_out_ab[j])
        else:
            x = x + short_conv_mixer(h, w_in_c[j], w_conv[j], b_conv[j], w_out_c[j])
        x = x + 0.5 * swiglu(rmsnorm(x, g_ffn2[i]), w_ffn2_in[i], w_ffn2_out[i])
        gate = jax.nn.sigmoid(rmsnorm(x, g_ple[i]) @ w_ple_gate[i])
        x = x + gate * (p[i] @ w_ple[i])
    return rmsnorm(x, g_final)


def setup_inputs(seed: int = 0) -> dict:
    key = jax.random.key(seed)
    ks = jax.random.split(key, 24)

    def nrm(k, shape, scale):
        return jax.random.normal(k, shape, jnp.float32) * scale

    def gain(k, shape):
        return 1.0 + 0.02 * jax.random.normal(k, shape, jnp.float32)

    return {
        'x_prompt': nrm(ks[0], (BATCH, SEQ, D_MODEL), 1.0),
        'x_sample': nrm(ks[1], (DEC_BATCH, DEC_SEQ, D_MODEL), 1.0),
        'p_prompt': nrm(ks[2], (DEPTH, BATCH, SEQ, PLE_DIM), 1.0),
        'p_sample': nrm(ks[3], (DEPTH, DEC_BATCH, DEC_SEQ, PLE_DIM), 1.0),
        'g_ffn1': gain(ks[4], (DEPTH, D_MODEL)),
        'w_ffn1_in': nrm(ks[5], (DEPTH, D_MODEL, 2 * D_FF), D_MODEL ** -0.5),
        'w_ffn1_out': nrm(ks[6], (DEPTH, D_FF, D_MODEL), D_FF ** -0.5),
        'g_mix': gain(ks[7], (DEPTH, D_MODEL)),
        'w_in_ab': nrm(ks[8], (N_EVEN, D_MODEL, AB_IN), D_MODEL ** -0.5),
        'g_v': gain(ks[9], (N_EVEN, N_B_HEADS, B_HEAD_DIM)),
        'w_s': nrm(ks[10], (N_EVEN, N_B_HEADS, CHUNK, CHUNK), CHUNK ** -0.5),
        'b_s': 1.0 + 0.1 * jax.random.normal(ks[11], (N_EVEN, N_B_HEADS, CHUNK), jnp.float32),
        'w_out_ab': nrm(ks[12], (N_EVEN, A_WIDTH + B_WIDTH, D_MODEL), (A_WIDTH + B_WIDTH) ** -0.5),
        'w_in_c': nrm(ks[13], (N_ODD, D_MODEL, 3 * C_WIDTH), D_MODEL ** -0.5),
        'w_conv': nrm(ks[14], (N_ODD, CONV_WIDTH, C_WIDTH), CONV_WIDTH ** -0.5),
        'b_conv': nrm(ks[15], (N_ODD, C_WIDTH), 0.01),
        'w_out_c': nrm(ks[16], (N_ODD, C_WIDTH, D_MODEL), C_WIDTH ** -0.5),
        'g_ffn2': gain(ks[17], (DEPTH, D_MODEL)),
        'w_ffn2_in': nrm(ks[18], (DEPTH, D_MODEL, 2 * D_FF), D_MODEL ** -0.5),
        'w_ffn2_out': nrm(ks[19], (DEPTH, D_FF, D_MODEL), D_FF ** -0.5),
        'g_ple': gain(ks[20], (DEPTH, D_MODEL)),
        'w_ple_gate': nrm(ks[21], (DEPTH, D_MODEL, D_MODEL), D_MODEL ** -0.5),
        'w_ple': nrm(ks[22], (DEPTH, PLE_DIM, D_MODEL), PLE_DIM ** -0.5),
        'g_final': gain(ks[23], (D_MODEL,)),
    }


def reference(x_prompt, x_sample, p_prompt, p_sample, g_ffn1, w_ffn1_in, w_ffn1_out, g_mix,
              w_in_ab, g_v, w_s, b_s, w_out_ab, w_in_c, w_conv, b_conv, w_out_c,
              g_ffn2, w_ffn2_in, w_ffn2_out, g_ple, w_ple_gate, w_ple, g_final):
    weights = (g_ffn1, w_ffn1_in, w_ffn1_out, g_mix, w_in_ab, g_v, w_s, b_s, w_out_ab,
               w_in_c, w_conv, b_conv, w_out_c, g_ffn2, w_ffn2_in, w_ffn2_out,
               g_ple, w_ple_gate, w_ple, g_final)
    y_prompt = trunk(x_prompt, p_prompt, *weights)
    y_sample = trunk(x_sample, p_sample, *weights)
    return (y_prompt, y_sample)
```

```python
import functools
import math

import numpy as np
import jax
import jax.numpy as jnp
from jax import lax
from jax.experimental import pallas as pl
from jax.experimental.pallas import tpu as pltpu

EPS = 1e-6
A_HEAD_DIM = 64
BF16 = jnp.bfloat16
F32 = jnp.float32

V7X_VMEM_BYTES = 64 * 1024 * 1024
V7X_MXU_DIM = 256
VMEM_RESERVE_BYTES = 6 * 1024 * 1024

TOKEN_TILE = 512
FF_CHUNK = 512
CONV_IN_CHUNK = 512
DFT_ROW_TILE = 1024
DFT_K_CHUNK = 2048


def _vmem_limit(*nbytes):
    return int(min(sum(nbytes) + VMEM_RESERVE_BYTES, V7X_VMEM_BYTES - 2 * 1024 * 1024))


def _nbytes(shape, dtype):
    return int(np.prod(shape)) * jnp.dtype(dtype).itemsize


def _params(semantics, *nbytes):
    return pltpu.CompilerParams(dimension_semantics=semantics,
                                vmem_limit_bytes=_vmem_limit(*nbytes))


def _const_spec(block, index_map):
    return pl.BlockSpec(block, index_map, pipeline_mode=pl.Buffered(1))


def _rms(x, g):
    ms = jnp.mean(x * x, axis=-1, keepdims=True)
    return x * lax.rsqrt(ms + EPS) * g


def _dot(a, b):
    return jnp.dot(a, b, preferred_element_type=F32)


def _ffn_body(x_ref, g_ref, wg_ref, wu_ref, wo_ref, o_ref, h_ref):
    j = pl.program_id(1)

    @pl.when(j == 0)
    def _():
        h_ref[...] = _rms(x_ref[...], g_ref[...]).astype(BF16)

    h = h_ref[...]
    gate = _dot(h, wg_ref[...])
    up = _dot(h, wu_ref[...])
    act = (gate * jax.nn.sigmoid(gate) * up).astype(BF16)
    y = 0.5 * _dot(act, wo_ref[...])

    @pl.when(j == 0)
    def _():
        o_ref[...] = x_ref[...] + y

    @pl.when(j > 0)
    def _():
        o_ref[...] += y


def _ffn(x, gains, w_in, w_out, layer):
    t, d = x.shape
    f = w_out.shape[1]
    tm, tk = TOKEN_TILE, FF_CHUNK
    nk = f // tk
    assert t % tm == 0 and f % tk == 0
    return pl.pallas_call(
        _ffn_body,
        grid=(t // tm, nk),
        in_specs=[
            pl.BlockSpec((tm, d), lambda i, j: (i, 0)),
            pl.BlockSpec((None, 1, d), lambda i, j: (layer, 0, 0)),
            pl.BlockSpec((None, d, tk), lambda i, j: (layer, 0, j)),
            pl.BlockSpec((None, d, tk), lambda i, j: (layer, 0, j + nk)),
            pl.BlockSpec((None, tk, d), lambda i, j: (layer, j, 0)),
        ],
        out_specs=pl.BlockSpec((tm, d), lambda i, j: (i, 0)),
        out_shape=jax.ShapeDtypeStruct((t, d), F32),
        scratch_shapes=[pltpu.VMEM((tm, d), BF16)],
        compiler_params=_params(
            ("parallel", "arbitrary"),
            4 * _nbytes((tm, d), F32), _nbytes((tm, d), BF16),
            6 * _nbytes((d, tk), BF16), 3 * _nbytes((tm, tk), F32)),
        name="ffn",
    )(x, gains, w_in, w_in, w_out)


def _ple_body(x_ref, p_ref, g_ref, wgate_ref, wp_ref, gf_ref, o_ref, *, final):
    x = x_ref[...]
    h = _rms(x, g_ref[...]).astype(BF16)
    gate = jax.nn.sigmoid(_dot(h, wgate_ref[...]))
    emb = _dot(p_ref[...].astype(BF16), wp_ref[...])
    y = x + gate * emb
    if final:
        y = _rms(y, gf_ref[...])
    o_ref[...] = y


def _ple(x, p, gains, w_gate, w_p, g_final, layer, final):
    t, d = x.shape
    e = p.shape[-1]
    tm = TOKEN_TILE
    return pl.pallas_call(
        functools.partial(_ple_body, final=final),
        grid=(t // tm,),
        in_specs=[
            pl.BlockSpec((tm, d), lambda i: (i, 0)),
            pl.BlockSpec((None, tm, e), lambda i: (layer, i, 0)),
            pl.BlockSpec((None, 1, d), lambda i: (layer, 0, 0)),
            _const_spec((None, d, d), lambda i: (layer, 0, 0)),
            _const_spec((None, e, d), lambda i: (layer, 0, 0)),
            pl.BlockSpec((1, d), lambda i: (0, 0)),
        ],
        out_specs=pl.BlockSpec((tm, d), lambda i: (i, 0)),
        out_shape=jax.ShapeDtypeStruct((t, d), F32),
        compiler_params=_params(
            ("parallel",),
            4 * _nbytes((tm, d), F32), 2 * _nbytes((tm, e), F32),
            _nbytes((d, d), BF16), _nbytes((e, d), BF16), 3 * _nbytes((tm, d), F32)),
        name="ple",
    )(x, p, gains, w_gate, w_p, g_final)


def _channel_dft_table():
    n = A_HEAD_DIM
    heads = V7X_MXU_DIM // n
    idx = np.arange(n)
    ang = 2.0 * np.pi * ((idx[:, None] * idx[None, :]) % n) / n
    eye = np.eye(heads)
    cos = np.kron(eye, np.cos(ang)) / math.sqrt(n)
    sin = np.kron(eye, np.sin(ang)) / math.sqrt(n)
    return jnp.asarray(np.concatenate([cos, sin], axis=1), dtype=BF16)


def _head_mean_table(head_dim):
    heads = V7X_MXU_DIM // head_dim
    m = np.kron(np.eye(heads), np.full((head_dim, head_dim), 1.0 / head_dim))
    return jnp.asarray(m, dtype=BF16)


def _sequence_dft_tables(s):
    base = V7X_MXU_DIM
    pos = jnp.arange(s, dtype=jnp.int32)[None, :]
    lo = jnp.arange(base, dtype=jnp.int32)[:, None]
    hi = jnp.arange(s // base, dtype=jnp.int32)[:, None] * base
    step = F32(2.0 * math.pi / s)
    ang_lo = ((lo * pos) % s).astype(F32) * step
    ang_hi = ((hi * pos) % s).astype(F32) * step
    scale = F32(1.0 / math.sqrt(s))
    c_lo, s_lo = jnp.cos(ang_lo)[None], jnp.sin(ang_lo)[None]
    c_hi, s_hi = (jnp.cos(ang_hi) * scale)[:, None], (jnp.sin(ang_hi) * scale)[:, None]
    cos = (c_hi * c_lo - s_hi * s_lo).reshape(s, s)
    nsin = -(s_hi * c_lo + c_hi * s_lo).reshape(s, s)
    return cos.astype(BF16), nsin.astype(BF16)


def _mix_in_body(x_ref, g_ref, wa_ref, wu_ref, wv_ref, cs_ref, hm_ref, gv_ref,
                 p_ref, u_ref, vn_ref):
    a = wa_ref.shape[1]
    w = V7X_MXU_DIM
    h = _rms(x_ref[...], g_ref[...]).astype(BF16)
    za = _dot(h, wa_ref[...]).astype(BF16)
    for t in range(a // w):
        pcs = _dot(za[:, t * w:(t + 1) * w], cs_ref[...])
        p_ref[:, t * w:(t + 1) * w] = pcs[:, :w].astype(BF16)
        p_ref[:, a + t * w:a + (t + 1) * w] = pcs[:, w:].astype(BF16)
    u_ref[...] = jax.nn.gelu(_dot(h, wu_ref[...]))
    v = jax.nn.gelu(_dot(h, wv_ref[...]))
    sq = (v * v).astype(BF16)
    for t in range(v.shape[1] // w):
        cols = slice(t * w, (t + 1) * w)
        ms = _dot(sq[:, cols], hm_ref[...])
        vn_ref[:, cols] = (v[:, cols] * lax.rsqrt(ms + EPS) * gv_ref[:, cols]).astype(BF16)


def _mix_in(x, gains, w_in, cs_tab, hm_tab, g_v, layer, j):
    t, d = x.shape
    a = d // 2
    b = d - a
    tm = TOKEN_TILE
    assert w_in.shape[2] == a + 2 * b and a == b
    return pl.pallas_call(
        _mix_in_body,
        grid=(t // tm,),
        in_specs=[
            pl.BlockSpec((tm, d), lambda i: (i, 0)),
            pl.BlockSpec((None, 1, d), lambda i: (layer, 0, 0)),
            _const_spec((None, d, a), lambda i: (j, 0, 0)),
            _const_spec((None, d, b), lambda i: (j, 0, 1)),
            _const_spec((None, d, b), lambda i: (j, 0, 2)),
            _const_spec(cs_tab.shape, lambda i: (0, 0)),
            _const_spec(hm_tab.shape, lambda i: (0, 0)),
            pl.BlockSpec((None, 1, b), lambda i: (j, 0, 0)),
        ],
        out_specs=[
            pl.BlockSpec((tm, 2 * a), lambda i: (i, 0)),
            pl.BlockSpec((tm, b), lambda i: (i, 0)),
            pl.BlockSpec((tm, b), lambda i: (i, 0)),
        ],
        out_shape=[
            jax.ShapeDtypeStruct((t, 2 * a), BF16),
            jax.ShapeDtypeStruct((t, b), F32),
            jax.ShapeDtypeStruct((t, b), BF16),
        ],
        compiler_params=_params(
            ("parallel",),
            2 * _nbytes((tm, d), F32), 3 * _nbytes((d, a), BF16),
            2 * _nbytes((tm, 2 * a), BF16), 2 * _nbytes((tm, b), F32),
            2 * _nbytes((tm, b), BF16), 4 * _nbytes((tm, a), F32)),
        name="mix_in",
    )(x, gains, w_in, w_in, w_in, cs_tab, hm_tab, g_v)


def _seq_dft_body(cos_ref, nsin_ref, p_ref, o_ref, acc_ref):
    a = o_ref.shape[-1]
    kk = pl.program_id(2)
    part = _dot(cos_ref[...], p_ref[:, :a]) + _dot(nsin_ref[...], p_ref[:, a:])

    @pl.when(kk == 0)
    def _():
        acc_ref[...] = part

    @pl.when(kk > 0)
    def _():
        acc_ref[...] += part

    @pl.when(kk == pl.num_programs(2) - 1)
    def _():
        o_ref[...] = acc_ref[...].astype(BF16)


def _seq_dft(p, cos_tab, nsin_tab):
    bsz, s, a2 = p.shape
    a = a2 // 2
    tr = min(DFT_ROW_TILE, s)
    tc = min(DFT_K_CHUNK, s)
    return pl.pallas_call(
        _seq_dft_body,
        grid=(bsz, s // tr, s // tc),
        in_specs=[
            pl.BlockSpec((tr, tc), lambda b, r, k: (r, k)),
            pl.BlockSpec((tr, tc), lambda b, r, k: (r, k)),
            pl.BlockSpec((None, tc, a2), lambda b, r, k: (b, k, 0)),
        ],
        out_specs=pl.BlockSpec((None, tr, a), lambda b, r, k: (b, r, 0)),
        out_shape=jax.ShapeDtypeStruct((bsz, s, a), BF16),
        scratch_shapes=[pltpu.VMEM((tr, a), F32)],
        compiler_params=_params(
            ("parallel", "parallel", "arbitrary"),
            4 * _nbytes((tr, tc), BF16), 2 * _nbytes((tc, a2), BF16),
            2 * _nbytes((tr, a), BF16), 3 * _nbytes((tr, a), F32)),
        name="seq_dft",
    )(cos_tab, nsin_tab, p)


def _mix_out_body(x_ref, ya_ref, u_ref, vn_ref, ws_ref, bias_ref, wo_ref, o_ref, y_ref,
                  *, head_dim):
    tm, a = ya_ref.shape
    chunk = ws_ref.shape[1]
    w = V7X_MXU_DIM
    heads = w // head_dim
    lane_head = lax.broadcasted_iota(jnp.int32, (chunk, w), 1) // head_dim
    y_ref[:, :a] = ya_ref[...]
    for c in range(tm // chunk):
        rows = slice(c * chunk, (c + 1) * chunk)
        for q in range(vn_ref.shape[1] // w):
            cols = slice(q * w, (q + 1) * w)
            stacked = _dot(ws_ref[q * heads * chunk:(q + 1) * heads * chunk, :],
                           vn_ref[rows, cols])
            mixed = stacked[:chunk]
            for r in range(1, heads):
                mixed = jnp.where(lane_head == r, stacked[r * chunk:(r + 1) * chunk], mixed)
            yb = u_ref[rows, cols] * (mixed + bias_ref[:, cols])
            y_ref[rows, a + q * w:a + (q + 1) * w] = yb.astype(BF16)
    o_ref[...] = x_ref[...] + _dot(y_ref[...], wo_ref[...])


def _mix_out(x, ya, u, vn, w_s, bias, w_out, j, head_dim):
    t, d = x.shape
    a = ya.shape[1]
    b = u.shape[1]
    chunk = w_s.shape[2]
    tm = TOKEN_TILE
    assert tm % chunk == 0 and b % V7X_MXU_DIM == 0 and V7X_MXU_DIM % head_dim == 0
    return pl.pallas_call(
        functools.partial(_mix_out_body, head_dim=head_dim),
        grid=(t // tm,),
        in_specs=[
            pl.BlockSpec((tm, d), lambda i: (i, 0)),
            pl.BlockSpec((tm, a), lambda i: (i, 0)),
            pl.BlockSpec((tm, b), lambda i: (i, 0)),
            pl.BlockSpec((tm, b), lambda i: (i, 0)),
            _const_spec((None,) + w_s.shape[1:], lambda i: (j, 0, 0)),
            _const_spec((None,) + bias.shape[1:], lambda i: (j, 0, 0)),
            _const_spec((None, a + b, d), lambda i: (j, 0, 0)),
        ],
        out_specs=pl.BlockSpec((tm, d), lambda i: (i, 0)),
        out_shape=jax.ShapeDtypeStruct((t, d), F32),
        scratch_shapes=[pltpu.VMEM((tm, a + b), BF16)],
        compiler_params=_params(
            ("parallel",),
            4 * _nbytes((tm, d), F32), 2 * _nbytes((tm, a), BF16),
            2 * _nbytes((tm, b), F32), 2 * _nbytes((tm, b), BF16),
            _nbytes(w_s.shape[1:], BF16), _nbytes(bias.shape[1:], F32),
            _nbytes((a + b, d), BF16), _nbytes((tm, a + b), BF16),
            2 * _nbytes((tm, d), F32)),
        name="mix_out",
    )(x, ya, u, vn, w_s, bias, w_out)


def _conv_in_body(x_ref, g_ref, wb_ref, wc_ref, wx_ref, gb_ref, z_ref, h_ref):
    @pl.when(pl.program_id(1) == 0)
    def _():
        h_ref[...] = _rms(x_ref[...], g_ref[...]).astype(BF16)

    h = h_ref[...]
    gb_ref[...] = _dot(h, wb_ref[...]).astype(BF16)
    z_ref[...] = (_dot(h, wc_ref[...]) * _dot(h, wx_ref[...])).astype(BF16)


def _conv_in(x, gains, w_in, layer, j):
    t, d = x.shape
    c = w_in.shape[2] // 3
    tm, tn = TOKEN_TILE, CONV_IN_CHUNK
    nn = c // tn
    return pl.pallas_call(
        _conv_in_body,
        grid=(t // tm, nn),
        in_specs=[
            pl.BlockSpec((tm, d), lambda i, n: (i, 0)),
            pl.BlockSpec((None, 1, d), lambda i, n: (layer, 0, 0)),
            pl.BlockSpec((None, d, tn), lambda i, n: (j, 0, n)),
            pl.BlockSpec((None, d, tn), lambda i, n: (j, 0, n + nn)),
            pl.BlockSpec((None, d, tn), lambda i, n: (j, 0, n + 2 * nn)),
        ],
        out_specs=[
            pl.BlockSpec((tm, tn), lambda i, n: (i, n)),
            pl.BlockSpec((tm, tn), lambda i, n: (i, n)),
        ],
        out_shape=[jax.ShapeDtypeStruct((t, c), BF16), jax.ShapeDtypeStruct((t, c), BF16)],
        scratch_shapes=[pltpu.VMEM((tm, d), BF16)],
        compiler_params=_params(
            ("parallel", "arbitrary"),
            2 * _nbytes((tm, d), F32), _nbytes((tm, d), BF16),
            6 * _nbytes((d, tn), BF16), 4 * _nbytes((tm, tn), BF16),
            3 * _nbytes((tm, tn), F32)),
        name="conv_in",
    )(x, gains, w_in, w_in, w_in)


def _conv_out_body(x_ref, z_ref, zp_ref, zn_ref, gb_ref, wc_ref, bc_ref, wo_ref, o_ref,
                   *, tiles_per_seq):
    tm = z_ref.shape[0]
    i = pl.program_id(0) % tiles_per_seq
    z = z_ref[...].astype(F32)
    halo = zp_ref.shape[0]
    before = jnp.where(i == 0, 0.0, zp_ref[...].astype(F32)[halo - 1:halo])
    after = jnp.where(i == tiles_per_seq - 1, 0.0, zn_ref[...].astype(F32)[0:1])
    row = lax.broadcasted_iota(jnp.int32, z.shape, 0)
    z_prev = jnp.where(row == 0, before, pltpu.roll(z, 1, axis=0))
    z_next = jnp.where(row == tm - 1, after, pltpu.roll(z, tm - 1, axis=0))
    conv = z_prev * wc_ref[0:1] + z * wc_ref[1:2] + z_next * wc_ref[2:3] + bc_ref[...]
    y = (gb_ref[...].astype(F32) * conv).astype(BF16)
    o_ref[...] = x_ref[...] + _dot(y, wo_ref[...])


def _conv_out(x, z, gb, w_conv, b_conv, w_out, j, seq):
    t, d = x.shape
    c = z.shape[1]
    tm = TOKEN_TILE
    halo = 16
    assert seq % tm == 0 and tm % halo == 0
    per = tm // halo
    last = t // halo - 1
    return pl.pallas_call(
        functools.partial(_conv_out_body, tiles_per_seq=seq // tm),
        grid=(t // tm,),
        in_specs=[
            pl.BlockSpec((tm, d), lambda i: (i, 0)),
            pl.BlockSpec((tm, c), lambda i: (i, 0)),
            pl.BlockSpec((halo, c), lambda i: (jnp.maximum(i * per - 1, 0), 0)),
            pl.BlockSpec((halo, c), lambda i: (jnp.minimum((i + 1) * per, last), 0)),
            pl.BlockSpec((tm, c), lambda i: (i, 0)),
            pl.BlockSpec((None,) + w_conv.shape[1:], lambda i: (j, 0, 0)),
            pl.BlockSpec((None, 1, c), lambda i: (j, 0, 0)),
            _const_spec((None, c, d), lambda i: (j, 0, 0)),
        ],
        out_specs=pl.BlockSpec((tm, d), lambda i: (i, 0)),
        out_shape=jax.ShapeDtypeStruct((t, d), F32),
        compiler_params=_params(
            ("parallel",),
            4 * _nbytes((tm, d), F32), 4 * _nbytes((tm, c), BF16),
            _nbytes((c, d), BF16), 5 * _nbytes((tm, c), F32)),
        name="conv_out",
    )(x, z, z, z, gb, w_conv, b_conv, w_out)


def _trunk(x, p, w, tabs):
    bsz, seq, d = x.shape
    depth = p.shape[0]
    t = bsz * seq
    x = x.reshape(t, d)
    p = p.reshape(depth, t, p.shape[-1])
    cos_tab, nsin_tab = _sequence_dft_tables(seq)
    head_dim = tabs["head_dim"]
    for i in range(depth):
        x = _ffn(x, w["g_ffn1"], w["w_ffn1_in"], w["w_ffn1_out"], i)
        j = i // 2
        if i % 2 == 0:
            pcs, u, vn = _mix_in(x, w["g_mix"], w["w_in_ab"], tabs["cs"], tabs["hm"],
                                 w["g_v"], i, j)
            ya = _seq_dft(pcs.reshape(bsz, seq, pcs.shape[-1]), cos_tab, nsin_tab)
            x = _mix_out(x, ya.reshape(t, ya.shape[-1]), u, vn, w["w_s"], w["bias_s"],
                         w["w_out_ab"], j, head_dim)
        else:
            gb, z = _conv_in(x, w["g_mix"], w["w_in_c"], i, j)
            x = _conv_out(x, z, gb, w["w_conv"], w["b_conv"], w["w_out_c"], j, seq)
        x = _ffn(x, w["g_ffn2"], w["w_ffn2_in"], w["w_ffn2_out"], i)
        x = _ple(x, p, w["g_ple"], w["w_ple_gate"], w["w_ple"], w["g_final"], i,
                 final=(i == depth - 1))
    return x.reshape(bsz, seq, d)


def kernel(x_prompt, x_sample, p_prompt, p_sample, g_ffn1, w_ffn1_in, w_ffn1_out, g_mix, w_in_ab, g_v, w_s, b_s, w_out_ab, w_in_c, w_conv, b_conv, w_out_c, g_ffn2, w_ffn2_in, w_ffn2_out, g_ple, w_ple_gate, w_ple, g_final):
    n_even, n_heads, head_dim = g_v.shape
    chunk = w_s.shape[-1]
    gain = lambda g: g[:, None, :]
    w = {
        "g_ffn1": gain(g_ffn1), "g_mix": gain(g_mix), "g_ffn2": gain(g_ffn2),
        "g_ple": gain(g_ple), "g_final": g_final[None, :],
        "g_v": g_v.reshape(n_even, 1, n_heads * head_dim),
        "w_ffn1_in": w_ffn1_in.astype(BF16), "w_ffn1_out": w_ffn1_out.astype(BF16),
        "w_ffn2_in": w_ffn2_in.astype(BF16), "w_ffn2_out": w_ffn2_out.astype(BF16),
        "w_in_ab": w_in_ab.astype(BF16), "w_out_ab": w_out_ab.astype(BF16),
        "w_s": w_s.astype(BF16).reshape(n_even, n_heads * chunk, chunk),
        "bias_s": jnp.repeat(jnp.swapaxes(b_s, 1, 2), head_dim, axis=2),
        "w_in_c": w_in_c.astype(BF16), "w_out_c": w_out_c.astype(BF16),
        "w_conv": w_conv, "b_conv": b_conv[:, None, :],
        "w_ple_gate": w_ple_gate.astype(BF16), "w_ple": w_ple.astype(BF16),
    }
    tabs = {"cs": _channel_dft_table(), "hm": _head_mean_table(head_dim), "head_dim": head_dim}
    return (_trunk(x_prompt, p_prompt, w, tabs), _trunk(x_sample, p_sample, w, tabs))
```

```python
import functools
import math

import numpy as np
import jax
import jax.numpy as jnp
from jax import lax
from jax.experimental import pallas as pl
from jax.experimental.pallas import tpu as pltpu

EPS = 1e-6
A_HEAD_DIM = 64
BF16 = jnp.bfloat16
F32 = jnp.float32

V7X_VMEM_BYTES = 64 * 1024 * 1024
V7X_MXU_DIM = 256
VMEM_RESERVE_BYTES = 6 * 1024 * 1024

TOKEN_TILE = 512
FF_CHUNK = 512
CONV_IN_CHUNK = 512
DFT_ROW_TILE = 1024
DFT_K_CHUNK = 2048


def _vmem_limit(*nbytes):
    return int(min(sum(nbytes) + VMEM_RESERVE_BYTES, V7X_VMEM_BYTES - 2 * 1024 * 1024))


def _nbytes(shape, dtype):
    return int(np.prod(shape)) * jnp.dtype(dtype).itemsize


def _params(semantics, *nbytes):
    return pltpu.CompilerParams(dimension_semantics=semantics,
                                vmem_limit_bytes=_vmem_limit(*nbytes))


def _const_spec(block, index_map):
    return pl.BlockSpec(block, index_map, pipeline_mode=pl.Buffered(1))


def _rms(x, g):
    ms = jnp.mean(x * x, axis=-1, keepdims=True)
    return x * lax.rsqrt(ms + EPS) * g


def _dot(a, b):
    return jnp.dot(a, b, preferred_element_type=F32)


def _ffn_body(x_ref, g_ref, wg_ref, wu_ref, wo_ref, o_ref, h_ref):
    j = pl.program_id(1)

    @pl.when(j == 0)
    def _():
        x = x_ref[...]
        h_ref[...] = _rms(x, g_ref[...]).astype(BF16)
        o_ref[...] = x

    h = h_ref[...]
    gate = _dot(h, wg_ref[...])
    up = _dot(h, wu_ref[...])
    act = (gate * jax.nn.sigmoid(gate) * up).astype(BF16)
    o_ref[...] += 0.5 * _dot(act, wo_ref[...])


def _ffn(x, gains, w_in, w_out, layer):
    t, d = x.shape
    f = w_out.shape[1]
    tm, tk = TOKEN_TILE, FF_CHUNK
    nk = f // tk
    assert t % tm == 0 and f % tk == 0
    return pl.pallas_call(
        _ffn_body,
        grid=(t // tm, nk),
        in_specs=[
            pl.BlockSpec((tm, d), lambda i, j: (i, 0)),
            pl.BlockSpec((None, 1, d), lambda i, j: (layer, 0, 0)),
            pl.BlockSpec((None, d, tk), lambda i, j: (layer, 0, j)),
            pl.BlockSpec((None, d, tk), lambda i, j: (layer, 0, j + nk)),
            pl.BlockSpec((None, tk, d), lambda i, j: (layer, j, 0)),
        ],
        out_specs=pl.BlockSpec((tm, d), lambda i, j: (i, 0)),
        out_shape=jax.ShapeDtypeStruct((t, d), F32),
        scratch_shapes=[pltpu.VMEM((tm, d), BF16)],
        compiler_params=_params(
            ("parallel", "arbitrary"),
            4 * _nbytes((tm, d), F32), _nbytes((tm, d), BF16),
            6 * _nbytes((d, tk), BF16), 3 * _nbytes((tm, tk), F32)),
        name="ffn",
    )(x, gains, w_in, w_in, w_out)


def _ple_body(x_ref, p_ref, g_ref, wgate_ref, wp_ref, gf_ref, o_ref, *, final):
    x = x_ref[...]
    h = _rms(x, g_ref[...]).astype(BF16)
    gate = jax.nn.sigmoid(_dot(h, wgate_ref[...]))
    emb = _dot(p_ref[...].astype(BF16), wp_ref[...])
    y = x + gate * emb
    if final:
        y = _rms(y, gf_ref[...])
    o_ref[...] = y


def _ple(x, p, gains, w_gate, w_p, g_final, layer, final):
    t, d = x.shape
    e = p.shape[-1]
    tm = TOKEN_TILE
    return pl.pallas_call(
        functools.partial(_ple_body, final=final),
        grid=(t // tm,),
        in_specs=[
            pl.BlockSpec((tm, d), lambda i: (i, 0)),
            pl.BlockSpec((None, tm, e), lambda i: (layer, i, 0)),
            pl.BlockSpec((None, 1, d), lambda i: (layer, 0, 0)),
            _const_spec((None, d, d), lambda i: (layer, 0, 0)),
            _const_spec((None, e, d), lambda i: (layer, 0, 0)),
            pl.BlockSpec((1, d), lambda i: (0, 0)),
        ],
        out_specs=pl.BlockSpec((tm, d), lambda i: (i, 0)),
        out_shape=jax.ShapeDtypeStruct((t, d), F32),
        compiler_params=_params(
            ("parallel",),
            4 * _nbytes((tm, d), F32), 2 * _nbytes((tm, e), F32),
            _nbytes((d, d), BF16), _nbytes((e, d), BF16), 3 * _nbytes((tm, d), F32)),
        name="ple",
    )(x, p, gains, w_gate, w_p, g_final)


def _channel_dft_table():
    n = A_HEAD_DIM
    heads = V7X_MXU_DIM // n
    idx = np.arange(n)
    ang = 2.0 * np.pi * ((idx[:, None] * idx[None, :]) % n) / n
    eye = np.eye(heads)
    cos = np.kron(eye, np.cos(ang)) / math.sqrt(n)
    sin = np.kron(eye, np.sin(ang)) / math.sqrt(n)
    return jnp.asarray(np.concatenate([cos, sin], axis=1), dtype=BF16)


def _head_mean_table(head_dim):
    heads = V7X_MXU_DIM // head_dim
    m = np.kron(np.eye(heads), np.full((head_dim, head_dim), 1.0 / head_dim))
    return jnp.asarray(m, dtype=BF16)


def _sequence_dft_tables(s):
    base = V7X_MXU_DIM
    pos = jnp.arange(s, dtype=jnp.int32)[None, :]
    lo = jnp.arange(base, dtype=jnp.int32)[:, None]
    hi = jnp.arange(s // base, dtype=jnp.int32)[:, None] * base
    step = F32(2.0 * math.pi / s)
    ang_lo = ((lo * pos) % s).astype(F32) * step
    ang_hi = ((hi * pos) % s).astype(F32) * step
    scale = F32(1.0 / math.sqrt(s))
    c_lo, s_lo = jnp.cos(ang_lo)[None], jnp.sin(ang_lo)[None]
    c_hi, s_hi = (jnp.cos(ang_hi) * scale)[:, None], (jnp.sin(ang_hi) * scale)[:, None]
    cos = (c_hi * c_lo - s_hi * s_lo).reshape(s, s)
    nsin = -(s_hi * c_lo + c_hi * s_lo).reshape(s, s)
    return cos.astype(BF16), nsin.astype(BF16)


def _mix_in_body(x_ref, g_ref, wa_ref, wu_ref, wv_ref, cs_ref, hm_ref, gv_ref,
                 p_ref, u_ref, vn_ref):
    a = wa_ref.shape[1]
    w = V7X_MXU_DIM
    h = _rms(x_ref[...], g_ref[...]).astype(BF16)
    za = _dot(h, wa_ref[...]).astype(BF16)
    for t in range(a // w):
        pcs = _dot(za[:, t * w:(t + 1) * w], cs_ref[...])
        p_ref[:, t * w:(t + 1) * w] = pcs[:, :w].astype(BF16)
        p_ref[:, a + t * w:a + (t + 1) * w] = pcs[:, w:].astype(BF16)
    u_ref[...] = jax.nn.gelu(_dot(h, wu_ref[...]))
    v = jax.nn.gelu(_dot(h, wv_ref[...]))
    sq = (v * v).astype(BF16)
    for t in range(v.shape[1] // w):
        cols = slice(t * w, (t + 1) * w)
        ms = _dot(sq[:, cols], hm_ref[...])
        vn_ref[:, cols] = (v[:, cols] * lax.rsqrt(ms + EPS) * gv_ref[:, cols]).astype(BF16)


def _mix_in(x, gains, w_in, cs_tab, hm_tab, g_v, layer, j):
    t, d = x.shape
    a = d // 2
    b = d - a
    tm = TOKEN_TILE
    assert w_in.shape[2] == a + 2 * b and a == b
    return pl.pallas_call(
        _mix_in_body,
        grid=(t // tm,),
        in_specs=[
            pl.BlockSpec((tm, d), lambda i: (i, 0)),
            pl.BlockSpec((None, 1, d), lambda i: (layer, 0, 0)),
            _const_spec((None, d, a), lambda i: (j, 0, 0)),
            _const_spec((None, d, b), lambda i: (j, 0, 1)),
            _const_spec((None, d, b), lambda i: (j, 0, 2)),
            _const_spec(cs_tab.shape, lambda i: (0, 0)),
            _const_spec(hm_tab.shape, lambda i: (0, 0)),
            pl.BlockSpec((None, 1, b), lambda i: (j, 0, 0)),
        ],
        out_specs=[
            pl.BlockSpec((tm, 2 * a), lambda i: (i, 0)),
            pl.BlockSpec((tm, b), lambda i: (i, 0)),
            pl.BlockSpec((tm, b), lambda i: (i, 0)),
        ],
        out_shape=[
            jax.ShapeDtypeStruct((t, 2 * a), BF16),
            jax.ShapeDtypeStruct((t, b), F32),
            jax.ShapeDtypeStruct((t, b), BF16),
        ],
        compiler_params=_params(
            ("parallel",),
            2 * _nbytes((tm, d), F32), 3 * _nbytes((d, a), BF16),
            2 * _nbytes((tm, 2 * a), BF16), 2 * _nbytes((tm, b), F32),
            2 * _nbytes((tm, b), BF16), 4 * _nbytes((tm, a), F32)),
        name="mix_in",
    )(x, gains, w_in, w_in, w_in, cs_tab, hm_tab, g_v)


def _seq_dft_body(cos_ref, nsin_ref, p_ref, o_ref, *acc, k_steps):
    a = o_ref.shape[-1]
    kk = pl.program_id(2)
    if k_steps > 1:
        acc_ref, = acc

        @pl.when(kk == 0)
        def _():
            acc_ref[...] = jnp.zeros_like(acc_ref)

    part = _dot(cos_ref[...], p_ref[:, :a]) + _dot(nsin_ref[...], p_ref[:, a:])
    if k_steps == 1:
        o_ref[...] = part.astype(BF16)
        return
    acc_ref[...] += part

    @pl.when(kk == k_steps - 1)
    def _():
        o_ref[...] = acc_ref[...].astype(BF16)


def _seq_dft(p, cos_tab, nsin_tab):
    bsz, s, a2 = p.shape
    a = a2 // 2
    tr = min(DFT_ROW_TILE, s)
    tc = min(DFT_K_CHUNK, s)
    k_steps = s // tc
    return pl.pallas_call(
        functools.partial(_seq_dft_body, k_steps=k_steps),
        grid=(bsz, s // tr, k_steps),
        in_specs=[
            pl.BlockSpec((tr, tc), lambda b, r, k: (r, k)),
            pl.BlockSpec((tr, tc), lambda b, r, k: (r, k)),
            pl.BlockSpec((None, tc, a2), lambda b, r, k: (b, k, 0)),
        ],
        out_specs=pl.BlockSpec((None, tr, a), lambda b, r, k: (b, r, 0)),
        out_shape=jax.ShapeDtypeStruct((bsz, s, a), BF16),
        scratch_shapes=[pltpu.VMEM((tr, a), F32)] if k_steps > 1 else [],
        compiler_params=_params(
            ("parallel", "parallel", "arbitrary"),
            4 * _nbytes((tr, tc), BF16), 2 * _nbytes((tc, a2), BF16),
            2 * _nbytes((tr, a), BF16), 3 * _nbytes((tr, a), F32)),
        name="seq_dft",
    )(cos_tab, nsin_tab, p)


def _mix_out_body(x_ref, ya_ref, u_ref, vn_ref, ws_ref, bias_ref, wo_ref, o_ref, y_ref,
                  *, head_dim):
    tm, a = ya_ref.shape
    chunk = ws_ref.shape[1]
    w = V7X_MXU_DIM
    heads = w // head_dim
    lane_head = lax.broadcasted_iota(jnp.int32, (chunk, w), 1) // head_dim
    y_ref[:, :a] = ya_ref[...]
    for c in range(tm // chunk):
        rows = slice(c * chunk, (c + 1) * chunk)
        for q in range(vn_ref.shape[1] // w):
            cols = slice(q * w, (q + 1) * w)
            stacked = _dot(ws_ref[q * heads * chunk:(q + 1) * heads * chunk, :],
                           vn_ref[rows, cols])
            mixed = stacked[:chunk]
            for r in range(1, heads):
                mixed = jnp.where(lane_head == r, stacked[r * chunk:(r + 1) * chunk], mixed)
            yb = u_ref[rows, cols] * (mixed + bias_ref[:, cols])
            y_ref[rows, a + q * w:a + (q + 1) * w] = yb.astype(BF16)
    o_ref[...] = x_ref[...] + _dot(y_ref[...], wo_ref[...])


def _mix_out(x, ya, u, vn, w_s, bias, w_out, j, head_dim):
    t, d = x.shape
    a = ya.shape[1]
    b = u.shape[1]
    chunk = w_s.shape[2]
    tm = TOKEN_TILE
    assert tm % chunk == 0 and b % V7X_MXU_DIM == 0 and V7X_MXU_DIM % head_dim == 0
    return pl.pallas_call(
        functools.partial(_mix_out_body, head_dim=head_dim),
        grid=(t // tm,),
        in_specs=[
            pl.BlockSpec((tm, d), lambda i: (i, 0)),
            pl.BlockSpec((tm, a), lambda i: (i, 0)),
            pl.BlockSpec((tm, b), lambda i: (i, 0)),
            pl.BlockSpec((tm, b), lambda i: (i, 0)),
            _const_spec((None,) + w_s.shape[1:], lambda i: (j, 0, 0)),
            _const_spec((None,) + bias.shape[1:], lambda i: (j, 0, 0)),
            _const_spec((None, a + b, d), lambda i: (j, 0, 0)),
        ],
        out_specs=pl.BlockSpec((tm, d), lambda i: (i, 0)),
        out_shape=jax.ShapeDtypeStruct((t, d), F32),
        scratch_shapes=[pltpu.VMEM((tm, a + b), BF16)],
        compiler_params=_params(
            ("parallel",),
            4 * _nbytes((tm, d), F32), 2 * _nbytes((tm, a), BF16),
            2 * _nbytes((tm, b), F32), 2 * _nbytes((tm, b), BF16),
            _nbytes(w_s.shape[1:], BF16), _nbytes(bias.shape[1:], F32),
            _nbytes((a + b, d), BF16), _nbytes((tm, a + b), BF16),
            2 * _nbytes((tm, d), F32)),
        name="mix_out",
    )(x, ya, u, vn, w_s, bias, w_out)


def _conv_in_body(x_ref, g_ref, wb_ref, wc_ref, wx_ref, gb_ref, z_ref, h_ref):
    @pl.when(pl.program_id(1) == 0)
    def _():
        h_ref[...] = _rms(x_ref[...], g_ref[...]).astype(BF16)

    h = h_ref[...]
    gb_ref[...] = _dot(h, wb_ref[...]).astype(BF16)
    z_ref[...] = (_dot(h, wc_ref[...]) * _dot(h, wx_ref[...])).astype(BF16)


def _conv_in(x, gains, w_in, layer, j):
    t, d = x.shape
    c = w_in.shape[2] // 3
    tm, tn = TOKEN_TILE, CONV_IN_CHUNK
    nn = c // tn
    return pl.pallas_call(
        _conv_in_body,
        grid=(t // tm, nn),
        in_specs=[
            pl.BlockSpec((tm, d), lambda i, n: (i, 0)),
            pl.BlockSpec((None, 1, d), lambda i, n: (layer, 0, 0)),
            pl.BlockSpec((None, d, tn), lambda i, n: (j, 0, n)),
            pl.BlockSpec((None, d, tn), lambda i, n: (j, 0, n + nn)),
            pl.BlockSpec((None, d, tn), lambda i, n: (j, 0, n + 2 * nn)),
        ],
        out_specs=[
            pl.BlockSpec((tm, tn), lambda i, n: (i, n)),
            pl.BlockSpec((tm, tn), lambda i, n: (i, n)),
        ],
        out_shape=[jax.ShapeDtypeStruct((t, c), BF16), jax.ShapeDtypeStruct((t, c), BF16)],
        scratch_shapes=[pltpu.VMEM((tm, d), BF16)],
        compiler_params=_params(
            ("parallel", "arbitrary"),
            2 * _nbytes((tm, d), F32), _nbytes((tm, d), BF16),
            6 * _nbytes((d, tn), BF16), 4 * _nbytes((tm, tn), BF16),
            3 * _nbytes((tm, tn), F32)),
        name="conv_in",
    )(x, gains, w_in, w_in, w_in)


def _conv_out_body(x_ref, z_ref, zp_ref, zn_ref, gb_ref, wc_ref, bc_ref, wo_ref, o_ref,
                   *, tiles_per_seq):
    tm = z_ref.shape[0]
    i = pl.program_id(0) % tiles_per_seq
    z = z_ref[...].astype(F32)
    halo = zp_ref.shape[0]
    before = jnp.where(i == 0, 0.0, zp_ref[...].astype(F32)[halo - 1:halo])
    after = jnp.where(i == tiles_per_seq - 1, 0.0, zn_ref[...].astype(F32)[0:1])
    row = lax.broadcasted_iota(jnp.int32, z.shape, 0)
    z_prev = jnp.where(row == 0, before, pltpu.roll(z, 1, axis=0))
    z_next = jnp.where(row == tm - 1, after, pltpu.roll(z, tm - 1, axis=0))
    conv = z_prev * wc_ref[0:1] + z * wc_ref[1:2] + z_next * wc_ref[2:3] + bc_ref[...]
    y = (gb_ref[...].astype(F32) * conv).astype(BF16)
    o_ref[...] = x_ref[...] + _dot(y, wo_ref[...])


def _conv_out(x, z, gb, w_conv, b_conv, w_out, j, seq):
    t, d = x.shape
    c = z.shape[1]
    tm = TOKEN_TILE
    halo = 16
    assert seq % tm == 0 and tm % halo == 0
    per = tm // halo
    last = t // halo - 1
    return pl.pallas_call(
        functools.partial(_conv_out_body, tiles_per_seq=seq // tm),
        grid=(t // tm,),
        in_specs=[
            pl.BlockSpec((tm, d), lambda i: (i, 0)),
            pl.BlockSpec((tm, c), lambda i: (i, 0)),
            pl.BlockSpec((halo, c), lambda i: (jnp.maximum(i * per - 1, 0), 0)),
            pl.BlockSpec((halo, c), lambda i: (jnp.minimum((i + 1) * per, last), 0)),
            pl.BlockSpec((tm, c), lambda i: (i, 0)),
            pl.BlockSpec((None,) + w_conv.shape[1:], lambda i: (j, 0, 0)),
            pl.BlockSpec((None, 1, c), lambda i: (j, 0, 0)),
            _const_spec((None, c, d), lambda i: (j, 0, 0)),
        ],
        out_specs=pl.BlockSpec((tm, d), lambda i: (i, 0)),
        out_shape=jax.ShapeDtypeStruct((t, d), F32),
        compiler_params=_params(
            ("parallel",),
            4 * _nbytes((tm, d), F32), 4 * _nbytes((tm, c), BF16),
            _nbytes((c, d), BF16), 5 * _nbytes((tm, c), F32)),
        name="conv_out",
    )(x, z, z, z, gb, w_conv, b_conv, w_out)


def _trunk(x, p, w, tabs):
    bsz, seq, d = x.shape
    depth = p.shape[0]
    t = bsz * seq
    x = x.reshape(t, d)
    p = p.reshape(depth, t, p.shape[-1])
    cos_tab, nsin_tab = _sequence_dft_tables(seq)
    head_dim = tabs["head_dim"]
    for i in range(depth):
        x = _ffn(x, w["g_ffn1"], w["w_ffn1_in"], w["w_ffn1_out"], i)
        j = i // 2
        if i % 2 == 0:
            pcs, u, vn = _mix_in(x, w["g_mix"], w["w_in_ab"], tabs["cs"], tabs["hm"],
                                 w["g_v"], i, j)
            ya = _seq_dft(pcs.reshape(bsz, seq, pcs.shape[-1]), cos_tab, nsin_tab)
            x = _mix_out(x, ya.reshape(t, ya.shape[-1]), u, vn, w["w_s"], w["bias_s"],
                         w["w_out_ab"], j, head_dim)
        else:
            gb, z = _conv_in(x, w["g_mix"], w["w_in_c"], i, j)
            x = _conv_out(x, z, gb, w["w_conv"], w["b_conv"], w["w_out_c"], j, seq)
        x = _ffn(x, w["g_ffn2"], w["w_ffn2_in"], w["w_ffn2_out"], i)
        x = _ple(x, p, w["g_ple"], w["w_ple_gate"], w["w_ple"], w["g_final"], i,
                 final=(i == depth - 1))
    return x.reshape(bsz, seq, d)


def kernel(x_prompt, x_sample, p_prompt, p_sample, g_ffn1, w_ffn1_in, w_ffn1_out, g_mix, w_in_ab, g_v, w_s, b_s, w_out_ab, w_in_c, w_conv, b_conv, w_out_c, g_ffn2, w_ffn2_in, w_ffn2_out, g_ple, w_ple_gate, w_ple, g_final):
    n_even, n_heads, head_dim = g_v.shape
    chunk = w_s.shape[-1]
    gain = lambda g: g[:, None, :]
    w = {
        "g_ffn1": gain(g_ffn1), "g_mix": gain(g_mix), "g_ffn2": gain(g_ffn2),
        "g_ple": gain(g_ple), "g_final": g_final[None, :],
        "g_v": g_v.reshape(n_even, 1, n_heads * head_dim),
        "w_ffn1_in": w_ffn1_in.astype(BF16), "w_ffn1_out": w_ffn1_out.astype(BF16),
        "w_ffn2_in": w_ffn2_in.astype(BF16), "w_ffn2_out": w_ffn2_out.astype(BF16),
        "w_in_ab": w_in_ab.astype(BF16), "w_out_ab": w_out_ab.astype(BF16),
        "w_s": w_s.astype(BF16).reshape(n_even, n_heads * chunk, chunk),
        "bias_s": jnp.repeat(jnp.swapaxes(b_s, 1, 2), head_dim, axis=2),
        "w_in_c": w_in_c.astype(BF16), "w_out_c": w_out_c.astype(BF16),
        "w_conv": w_conv, "b_conv": b_conv[:, None, :],
        "w_ple_gate": w_ple_gate.astype(BF16), "w_ple": w_ple.astype(BF16),
    }
    tabs = {"cs": _channel_dft_table(), "hm": _head_mean_table(head_dim), "head_dim": head_dim}
    return (_trunk(x_prompt, p_prompt, w, tabs), _trunk(x_sample, p_sample, w, tabs))
```

```python
import functools
import math

import numpy as np
import jax
import jax.numpy as jnp
from jax import lax
from jax.experimental import pallas as pl
from jax.experimental.pallas import tpu as pltpu

EPS = 1e-6
A_HEAD_DIM = 64
BF16 = jnp.bfloat16
F32 = jnp.float32

V7X_VMEM_BYTES = 64 * 1024 * 1024
V7X_MXU_DIM = 256
V7X_LANES = 128
VMEM_RESERVE_BYTES = 6 * 1024 * 1024

TOKEN_TILE = 512
FFN_TOKEN_TILE = 1024
FF_CHUNK = 512
CONV_IN_CHUNK = 512
DFT_ROW_TILE = 1024
DFT_K_CHUNK = 1024


def _vmem_limit(*nbytes):
    return int(min(sum(nbytes) + VMEM_RESERVE_BYTES, V7X_VMEM_BYTES - 2 * 1024 * 1024))


def _nbytes(shape, dtype):
    return int(np.prod(shape)) * jnp.dtype(dtype).itemsize


def _params(semantics, *nbytes):
    return pltpu.CompilerParams(dimension_semantics=semantics,
                                vmem_limit_bytes=_vmem_limit(*nbytes))


def _const_spec(block, index_map):
    return pl.BlockSpec(block, index_map, pipeline_mode=pl.Buffered(1))


def _rms(x, g):
    ms = jnp.mean(x * x, axis=-1, keepdims=True)
    return x * lax.rsqrt(ms + EPS) * g


def _dot(a, b):
    return jnp.dot(a, b, preferred_element_type=F32)


def _ffn_body(x_ref, g_ref, wg_ref, wu_ref, wo_ref, o_ref, h_ref):
    j = pl.program_id(1)

    @pl.when(j == 0)
    def _():
        x = x_ref[...]
        h_ref[...] = _rms(x, g_ref[...]).astype(BF16)
        o_ref[...] = x

    h = h_ref[...]
    gate = _dot(h, wg_ref[...])
    up = _dot(h, wu_ref[...])
    act = (gate * jax.nn.sigmoid(gate) * up).astype(BF16)
    o_ref[...] += 0.5 * _dot(act, wo_ref[...])


def _ffn(x, gains, w_in, w_out, layer):
    t, d = x.shape
    f = w_out.shape[1]
    tm, tk = FFN_TOKEN_TILE, FF_CHUNK
    nk = f // tk
    assert t % tm == 0 and f % tk == 0
    return pl.pallas_call(
        _ffn_body,
        grid=(t // tm, nk),
        in_specs=[
            pl.BlockSpec((tm, d), lambda i, j: (i, 0)),
            pl.BlockSpec((None, 1, d), lambda i, j: (layer, 0, 0)),
            pl.BlockSpec((None, d, tk), lambda i, j: (layer, 0, j)),
            pl.BlockSpec((None, d, tk), lambda i, j: (layer, 0, j + nk)),
            pl.BlockSpec((None, tk, d), lambda i, j: (layer, j, 0)),
        ],
        out_specs=pl.BlockSpec((tm, d), lambda i, j: (i, 0)),
        out_shape=jax.ShapeDtypeStruct((t, d), F32),
        scratch_shapes=[pltpu.VMEM((tm, d), BF16)],
        compiler_params=_params(
            ("parallel", "arbitrary"),
            4 * _nbytes((tm, d), F32), _nbytes((tm, d), BF16),
            6 * _nbytes((d, tk), BF16), 3 * _nbytes((tm, tk), F32)),
        name="ffn",
    )(x, gains, w_in, w_in, w_out)


def _ple_body(x_ref, p_ref, g_ref, wgate_ref, wp_ref, gf_ref, o_ref, *, final):
    x = x_ref[...]
    h = _rms(x, g_ref[...]).astype(BF16)
    gate = jax.nn.sigmoid(_dot(h, wgate_ref[...]))
    emb = _dot(p_ref[...].astype(BF16), wp_ref[...])
    y = x + gate * emb
    if final:
        y = _rms(y, gf_ref[...])
    o_ref[...] = y


def _ple(x, p, gains, w_gate, w_p, g_final, layer, final):
    t, d = x.shape
    e = p.shape[-1]
    tm = TOKEN_TILE
    return pl.pallas_call(
        functools.partial(_ple_body, final=final),
        grid=(t // tm,),
        in_specs=[
            pl.BlockSpec((tm, d), lambda i: (i, 0)),
            pl.BlockSpec((None, tm, e), lambda i: (layer, i, 0)),
            pl.BlockSpec((None, 1, d), lambda i: (layer, 0, 0)),
            _const_spec((None, d, d), lambda i: (layer, 0, 0)),
            _const_spec((None, e, d), lambda i: (layer, 0, 0)),
            pl.BlockSpec((1, d), lambda i: (0, 0)),
        ],
        out_specs=pl.BlockSpec((tm, d), lambda i: (i, 0)),
        out_shape=jax.ShapeDtypeStruct((t, d), F32),
        compiler_params=_params(
            ("parallel",),
            4 * _nbytes((tm, d), F32), 2 * _nbytes((tm, e), F32),
            _nbytes((d, d), BF16), _nbytes((e, d), BF16), 3 * _nbytes((tm, d), F32)),
        name="ple",
    )(x, p, gains, w_gate, w_p, g_final)


def _channel_dft_table():
    n = A_HEAD_DIM
    heads = V7X_MXU_DIM // n
    idx = np.arange(n)
    ang = 2.0 * np.pi * ((idx[:, None] * idx[None, :]) % n) / n
    eye = np.eye(heads)
    cos = np.kron(eye, np.cos(ang)) / math.sqrt(n)
    sin = np.kron(eye, np.sin(ang)) / math.sqrt(n)
    return jnp.asarray(np.concatenate([cos, sin], axis=1), dtype=BF16)


def _head_mean_table(head_dim):
    heads = V7X_MXU_DIM // head_dim
    m = np.kron(np.eye(heads), np.full((head_dim, head_dim), 1.0 / head_dim))
    return jnp.asarray(m, dtype=BF16)


def _sequence_dft_tables(s, parity):
    base = V7X_MXU_DIM
    half = s // 2
    pos = (2 * jnp.arange(half, dtype=jnp.int32) + parity)[None, :]
    lo = jnp.arange(base, dtype=jnp.int32)[:, None]
    hi = jnp.arange(half // base, dtype=jnp.int32)[:, None] * base
    step = F32(2.0 * math.pi / s)
    ang_lo = ((lo * pos) % s).astype(F32) * step
    ang_hi = ((hi * pos) % s).astype(F32) * step
    scale = F32(1.0 / math.sqrt(s))
    c_lo, s_lo = jnp.cos(ang_lo)[None], jnp.sin(ang_lo)[None]
    c_hi, s_hi = (jnp.cos(ang_hi) * scale)[:, None], (jnp.sin(ang_hi) * scale)[:, None]
    cos = (c_hi * c_lo - s_hi * s_lo).reshape(half, half)
    nsin = -(s_hi * c_lo + c_hi * s_lo).reshape(half, half)
    return cos.astype(BF16), nsin.astype(BF16)


def _mix_in_body(x_ref, g_ref, wa_ref, wu_ref, wv_ref, cs_ref, hm_ref, gv_ref,
                 pe_ref, po_ref, u_ref, vn_ref, p_ref):
    a = wa_ref.shape[1]
    w = V7X_MXU_DIM
    half = pe_ref.shape[0]
    h = _rms(x_ref[...], g_ref[...]).astype(BF16)
    za = _dot(h, wa_ref[...]).astype(BF16)
    lanes = p_ref.shape[-1]
    per_tile = w // lanes
    for t in range(a // w):
        pcs = _dot(za[:, t * w:(t + 1) * w], cs_ref[...])
        for c in range(2 * per_tile):
            n = (c // per_tile) * (a // lanes) + t * per_tile + c % per_tile
            p_ref[n] = pcs[:, c * lanes:(c + 1) * lanes]
    for n in range(p_ref.shape[0]):
        cols = slice(n * lanes, (n + 1) * lanes)
        pe_ref[:, cols] = p_ref[n, pl.ds(0, half, stride=2), :].astype(BF16)
        po_ref[:, cols] = p_ref[n, pl.ds(1, half, stride=2), :].astype(BF16)
    u_ref[...] = jax.nn.gelu(_dot(h, wu_ref[...]))
    v = jax.nn.gelu(_dot(h, wv_ref[...]))
    sq = (v * v).astype(BF16)
    for t in range(v.shape[1] // w):
        cols = slice(t * w, (t + 1) * w)
        ms = _dot(sq[:, cols], hm_ref[...])
        vn_ref[:, cols] = (v[:, cols] * lax.rsqrt(ms + EPS) * gv_ref[:, cols]).astype(BF16)


def _mix_in(x, gains, w_in, cs_tab, hm_tab, g_v, layer, j):
    t, d = x.shape
    a = d // 2
    b = d - a
    tm = TOKEN_TILE
    assert w_in.shape[2] == a + 2 * b and a == b
    return pl.pallas_call(
        _mix_in_body,
        grid=(t // tm,),
        in_specs=[
            pl.BlockSpec((tm, d), lambda i: (i, 0)),
            pl.BlockSpec((None, 1, d), lambda i: (layer, 0, 0)),
            _const_spec((None, d, a), lambda i: (j, 0, 0)),
            _const_spec((None, d, b), lambda i: (j, 0, 1)),
            _const_spec((None, d, b), lambda i: (j, 0, 2)),
            _const_spec(cs_tab.shape, lambda i: (0, 0)),
            _const_spec(hm_tab.shape, lambda i: (0, 0)),
            pl.BlockSpec((None, 1, b), lambda i: (j, 0, 0)),
        ],
        out_specs=[
            pl.BlockSpec((tm // 2, 2 * a), lambda i: (i, 0)),
            pl.BlockSpec((tm // 2, 2 * a), lambda i: (i, 0)),
            pl.BlockSpec((tm, b), lambda i: (i, 0)),
            pl.BlockSpec((tm, b), lambda i: (i, 0)),
        ],
        out_shape=[
            jax.ShapeDtypeStruct((t // 2, 2 * a), BF16),
            jax.ShapeDtypeStruct((t // 2, 2 * a), BF16),
            jax.ShapeDtypeStruct((t, b), F32),
            jax.ShapeDtypeStruct((t, b), BF16),
        ],
        scratch_shapes=[pltpu.VMEM((2 * a // V7X_LANES, tm, V7X_LANES), F32)],
        compiler_params=_params(
            ("parallel",),
            2 * _nbytes((tm, d), F32), 3 * _nbytes((d, a), BF16),
            2 * _nbytes((tm, 2 * a), BF16), 2 * _nbytes((tm, b), F32),
            2 * _nbytes((tm, b), BF16), _nbytes((tm, 2 * a), F32),
            4 * _nbytes((tm, a), F32)),
        name="mix_in",
    )(x, gains, w_in, w_in, w_in, cs_tab, hm_tab, g_v)


def _seq_dft_body(ce_ref, se_ref, co_ref, so_ref, pe_ref, po_ref, o_ref, *acc, k_steps):
    a = o_ref.shape[-1]
    kk = pl.program_id(2)
    if k_steps > 1:
        acc_e, acc_o = acc

        @pl.when(kk == 0)
        def _():
            acc_e[...] = jnp.zeros_like(acc_e)
            acc_o[...] = jnp.zeros_like(acc_o)

    even = _dot(ce_ref[...], pe_ref[:, :a]) + _dot(se_ref[...], pe_ref[:, a:])
    odd = _dot(co_ref[...], po_ref[:, :a]) + _dot(so_ref[...], po_ref[:, a:])
    if k_steps == 1:
        o_ref[0] = (even + odd).astype(BF16)
        o_ref[1] = (even - odd).astype(BF16)
        return
    acc_e[...] += even
    acc_o[...] += odd

    @pl.when(kk == k_steps - 1)
    def _():
        o_ref[0] = (acc_e[...] + acc_o[...]).astype(BF16)
        o_ref[1] = (acc_e[...] - acc_o[...]).astype(BF16)


def _seq_dft(pe, po, tabs_even, tabs_odd):
    bsz, half, a2 = pe.shape
    a = a2 // 2
    tr = min(DFT_ROW_TILE, half)
    tc = min(DFT_K_CHUNK, half)
    k_steps = half // tc
    tab_spec = pl.BlockSpec((tr, tc), lambda b, r, k: (r, k))
    p_spec = pl.BlockSpec((None, tc, a2), lambda b, r, k: (b, k, 0))
    return pl.pallas_call(
        functools.partial(_seq_dft_body, k_steps=k_steps),
        grid=(bsz, half // tr, k_steps),
        in_specs=[tab_spec, tab_spec, tab_spec, tab_spec, p_spec, p_spec],
        out_specs=pl.BlockSpec((None, 2, tr, a), lambda b, r, k: (b, 0, r, 0)),
        out_shape=jax.ShapeDtypeStruct((bsz, 2, half, a), BF16),
        scratch_shapes=[pltpu.VMEM((tr, a), F32)] * 2 if k_steps > 1 else [],
        compiler_params=_params(
            ("parallel", "parallel", "arbitrary"),
            8 * _nbytes((tr, tc), BF16), 4 * _nbytes((tc, a2), BF16),
            2 * _nbytes((2, tr, a), BF16), 4 * _nbytes((tr, a), F32)),
        name="seq_dft",
    )(*tabs_even, *tabs_odd, pe, po)


def _mix_out_body(x_ref, ya_ref, u_ref, vn_ref, ws_ref, bias_ref, wo_ref, o_ref, y_ref,
                  *, head_dim):
    tm, a = ya_ref.shape
    chunk = ws_ref.shape[1]
    w = V7X_MXU_DIM
    heads = w // head_dim
    lane_head = lax.broadcasted_iota(jnp.int32, (chunk, w), 1) // head_dim
    y_ref[:, :a] = ya_ref[...]
    for c in range(tm // chunk):
        rows = slice(c * chunk, (c + 1) * chunk)
        for q in range(vn_ref.shape[1] // w):
            cols = slice(q * w, (q + 1) * w)
            stacked = _dot(ws_ref[q * heads * chunk:(q + 1) * heads * chunk, :],
                           vn_ref[rows, cols])
            mixed = stacked[:chunk]
            for r in range(1, heads):
                mixed = jnp.where(lane_head == r, stacked[r * chunk:(r + 1) * chunk], mixed)
            yb = u_ref[rows, cols] * (mixed + bias_ref[:, cols])
            y_ref[rows, a + q * w:a + (q + 1) * w] = yb.astype(BF16)
    o_ref[...] = x_ref[...] + _dot(y_ref[...], wo_ref[...])


def _mix_out(x, ya, u, vn, w_s, bias, w_out, j, head_dim):
    t, d = x.shape
    a = ya.shape[1]
    b = u.shape[1]
    chunk = w_s.shape[2]
    tm = TOKEN_TILE
    assert tm % chunk == 0 and b % V7X_MXU_DIM == 0 and V7X_MXU_DIM % head_dim == 0
    return pl.pallas_call(
        functools.partial(_mix_out_body, head_dim=head_dim),
        grid=(t // tm,),
        in_specs=[
            pl.BlockSpec((tm, d), lambda i: (i, 0)),
            pl.BlockSpec((tm, a), lambda i: (i, 0)),
            pl.BlockSpec((tm, b), lambda i: (i, 0)),
            pl.BlockSpec((tm, b), lambda i: (i, 0)),
            _const_spec((None,) + w_s.shape[1:], lambda i: (j, 0, 0)),
            _const_spec((None,) + bias.shape[1:], lambda i: (j, 0, 0)),
            _const_spec((None, a + b, d), lambda i: (j, 0, 0)),
        ],
        out_specs=pl.BlockSpec((tm, d), lambda i: (i, 0)),
        out_shape=jax.ShapeDtypeStruct((t, d), F32),
        scratch_shapes=[pltpu.VMEM((tm, a + b), BF16)],
        compiler_params=_params(
            ("parallel",),
            4 * _nbytes((tm, d), F32), 2 * _nbytes((tm, a), BF16),
            2 * _nbytes((tm, b), F32), 2 * _nbytes((tm, b), BF16),
            _nbytes(w_s.shape[1:], BF16), _nbytes(bias.shape[1:], F32),
            _nbytes((a + b, d), BF16), _nbytes((tm, a + b), BF16),
            2 * _nbytes((tm, d), F32)),
        name="mix_out",
    )(x, ya, u, vn, w_s, bias, w_out)


def _conv_in_body(x_ref, g_ref, wb_ref, wc_ref, wx_ref, gb_ref, z_ref, h_ref):
    @pl.when(pl.program_id(1) == 0)
    def _():
        h_ref[...] = _rms(x_ref[...], g_ref[...]).astype(BF16)

    h = h_ref[...]
    gb_ref[...] = _dot(h, wb_ref[...]).astype(BF16)
    z_ref[...] = (_dot(h, wc_ref[...]) * _dot(h, wx_ref[...])).astype(BF16)


def _conv_in(x, gains, w_in, layer, j):
    t, d = x.shape
    c = w_in.shape[2] // 3
    tm, tn = TOKEN_TILE, CONV_IN_CHUNK
    nn = c // tn
    return pl.pallas_call(
        _conv_in_body,
        grid=(t // tm, nn),
        in_specs=[
            pl.BlockSpec((tm, d), lambda i, n: (i, 0)),
            pl.BlockSpec((None, 1, d), lambda i, n: (layer, 0, 0)),
            pl.BlockSpec((None, d, tn), lambda i, n: (j, 0, n)),
            pl.BlockSpec((None, d, tn), lambda i, n: (j, 0, n + nn)),
            pl.BlockSpec((None, d, tn), lambda i, n: (j, 0, n + 2 * nn)),
        ],
        out_specs=[
            pl.BlockSpec((tm, tn), lambda i, n: (i, n)),
            pl.BlockSpec((tm, tn), lambda i, n: (i, n)),
        ],
        out_shape=[jax.ShapeDtypeStruct((t, c), BF16), jax.ShapeDtypeStruct((t, c), BF16)],
        scratch_shapes=[pltpu.VMEM((tm, d), BF16)],
        compiler_params=_params(
            ("parallel", "arbitrary"),
            2 * _nbytes((tm, d), F32), _nbytes((tm, d), BF16),
            6 * _nbytes((d, tn), BF16), 4 * _nbytes((tm, tn), BF16),
            3 * _nbytes((tm, tn), F32)),
        name="conv_in",
    )(x, gains, w_in, w_in, w_in)


def _conv_out_body(x_ref, z_ref, zp_ref, zn_ref, gb_ref, wc_ref, bc_ref, wo_ref, o_ref,
                   *, tiles_per_seq):
    tm = z_ref.shape[0]
    i = pl.program_id(0) % tiles_per_seq
    z = z_ref[...].astype(F32)
    halo = zp_ref.shape[0]
    before = jnp.where(i == 0, 0.0, zp_ref[...].astype(F32)[halo - 1:halo])
    after = jnp.where(i == tiles_per_seq - 1, 0.0, zn_ref[...].astype(F32)[0:1])
    row = lax.broadcasted_iota(jnp.int32, z.shape, 0)
    z_prev = jnp.where(row == 0, before, pltpu.roll(z, 1, axis=0))
    z_next = jnp.where(row == tm - 1, after, pltpu.roll(z, tm - 1, axis=0))
    conv = z_prev * wc_ref[0:1] + z * wc_ref[1:2] + z_next * wc_ref[2:3] + bc_ref[...]
    y = (gb_ref[...].astype(F32) * conv).astype(BF16)
    o_ref[...] = x_ref[...] + _dot(y, wo_ref[...])


def _conv_out(x, z, gb, w_conv, b_conv, w_out, j, seq):
    t, d = x.shape
    c = z.shape[1]
    tm = TOKEN_TILE
    halo = 16
    assert seq % tm == 0 and tm % halo == 0
    per = tm // halo
    last = t // halo - 1
    return pl.pallas_call(
        functools.partial(_conv_out_body, tiles_per_seq=seq // tm),
        grid=(t // tm,),
        in_specs=[
            pl.BlockSpec((tm, d), lambda i: (i, 0)),
            pl.BlockSpec((tm, c), lambda i: (i, 0)),
            pl.BlockSpec((halo, c), lambda i: (jnp.maximum(i * per - 1, 0), 0)),
            pl.BlockSpec((halo, c), lambda i: (jnp.minimum((i + 1) * per, last), 0)),
            pl.BlockSpec((tm, c), lambda i: (i, 0)),
            pl.BlockSpec((None,) + w_conv.shape[1:], lambda i: (j, 0, 0)),
            pl.BlockSpec((None, 1, c), lambda i: (j, 0, 0)),
            _const_spec((None, c, d), lambda i: (j, 0, 0)),
        ],
        out_specs=pl.BlockSpec((tm, d), lambda i: (i, 0)),
        out_shape=jax.ShapeDtypeStruct((t, d), F32),
        compiler_params=_params(
            ("parallel",),
            4 * _nbytes((tm, d), F32), 4 * _nbytes((tm, c), BF16),
            _nbytes((c, d), BF16), 5 * _nbytes((tm, c), F32)),
        name="conv_out",
    )(x, z, z, z, gb, w_conv, b_conv, w_out)


def _trunk(x, p, w, tabs):
    bsz, seq, d = x.shape
    depth = p.shape[0]
    t = bsz * seq
    x = x.reshape(t, d)
    p = p.reshape(depth, t, p.shape[-1])
    tabs_even, tabs_odd = _sequence_dft_tables(seq, 0), _sequence_dft_tables(seq, 1)
    head_dim = tabs["head_dim"]
    for i in range(depth):
        x = _ffn(x, w["g_ffn1"], w["w_ffn1_in"], w["w_ffn1_out"], i)
        j = i // 2
        if i % 2 == 0:
            pe, po, u, vn = _mix_in(x, w["g_mix"], w["w_in_ab"], tabs["cs"], tabs["hm"],
                                    w["g_v"], i, j)
            ya = _seq_dft(pe.reshape(bsz, seq // 2, pe.shape[-1]),
                          po.reshape(bsz, seq // 2, po.shape[-1]), tabs_even, tabs_odd)
            x = _mix_out(x, ya.reshape(t, ya.shape[-1]), u, vn, w["w_s"], w["bias_s"],
                         w["w_out_ab"], j, head_dim)
        else:
            gb, z = _conv_in(x, w["g_mix"], w["w_in_c"], i, j)
            x = _conv_out(x, z, gb, w["w_conv"], w["b_conv"], w["w_out_c"], j, seq)
        x = _ffn(x, w["g_ffn2"], w["w_ffn2_in"], w["w_ffn2_out"], i)
        x = _ple(x, p, w["g_ple"], w["w_ple_gate"], w["w_ple"], w["g_final"], i,
                 final=(i == depth - 1))
    return x.reshape(bsz, seq, d)


def kernel(x_prompt, x_sample, p_prompt, p_sample, g_ffn1, w_ffn1_in, w_ffn1_out, g_mix, w_in_ab, g_v, w_s, b_s, w_out_ab, w_in_c, w_conv, b_conv, w_out_c, g_ffn2, w_ffn2_in, w_ffn2_out, g_ple, w_ple_gate, w_ple, g_final):
    n_even, n_heads, head_dim = g_v.shape
    chunk = w_s.shape[-1]
    gain = lambda g: g[:, None, :]
    w = {
        "g_ffn1": gain(g_ffn1), "g_mix": gain(g_mix), "g_ffn2": gain(g_ffn2),
        "g_ple": gain(g_ple), "g_final": g_final[None, :],
        "g_v": g_v.reshape(n_even, 1, n_heads * head_dim),
        "w_ffn1_in": w_ffn1_in.astype(BF16), "w_ffn1_out": w_ffn1_out.astype(BF16),
        "w_ffn2_in": w_ffn2_in.astype(BF16), "w_ffn2_out": w_ffn2_out.astype(BF16),
        "w_in_ab": w_in_ab.astype(BF16), "w_out_ab": w_out_ab.astype(BF16),
        "w_s": w_s.astype(BF16).reshape(n_even, n_heads * chunk, chunk),
        "bias_s": jnp.repeat(jnp.swapaxes(b_s, 1, 2), head_dim, axis=2),
        "w_in_c": w_in_c.astype(BF16), "w_out_c": w_out_c.astype(BF16),
        "w_conv": w_conv, "b_conv": b_conv[:, None, :],
        "w_ple_gate": w_ple_gate.astype(BF16), "w_ple": w_ple.astype(BF16),
    }
    tabs = {"cs": _channel_dft_table(), "hm": _head_mean_table(head_dim), "head_dim": head_dim}
    return (_trunk(x_prompt, p_prompt, w, tabs), _trunk(x_sample, p_sample, w, tabs))
```

```python
import functools
import math

import numpy as np
import jax
import jax.numpy as jnp
from jax import lax
from jax.experimental import pallas as pl
from jax.experimental.pallas import tpu as pltpu

EPS = 1e-6
A_HEAD_DIM = 64
BF16 = jnp.bfloat16
F32 = jnp.float32

V7X_VMEM_BYTES = 64 * 1024 * 1024
V7X_MXU_DIM = 256
V7X_LANES = 128
VMEM_RESERVE_BYTES = 6 * 1024 * 1024

TOKEN_TILE = 1024
MIX_IN_TOKEN_TILE = 512
FF_CHUNK = 512
CONV_IN_CHUNK = 512
DFT_ROW_TILE = 1024
DFT_K_CHUNK = 1024


def _vmem_limit(*nbytes):
    return int(min(sum(nbytes) + VMEM_RESERVE_BYTES, V7X_VMEM_BYTES - 2 * 1024 * 1024))


def _nbytes(shape, dtype):
    return int(np.prod(shape)) * jnp.dtype(dtype).itemsize


def _params(semantics, *nbytes):
    return pltpu.CompilerParams(dimension_semantics=semantics,
                                vmem_limit_bytes=_vmem_limit(*nbytes))


def _const_spec(block, index_map):
    return pl.BlockSpec(block, index_map, pipeline_mode=pl.Buffered(1))


def _rms(x, g):
    ms = jnp.mean(x * x, axis=-1, keepdims=True)
    return x * lax.rsqrt(ms + EPS) * g


def _dot(a, b):
    return jnp.dot(a, b, preferred_element_type=F32)


def _ffn_body(x_ref, g_ref, wg_ref, wu_ref, wo_ref, o_ref, h_ref):
    j = pl.program_id(1)

    def half_swiglu(h):
        gate = _dot(h, wg_ref[...])
        up = _dot(h, wu_ref[...])
        act = (gate * jax.nn.sigmoid(gate) * up).astype(BF16)
        return 0.5 * _dot(act, wo_ref[...])

    @pl.when(j == 0)
    def _():
        x = x_ref[...]
        h = _rms(x, g_ref[...]).astype(BF16)
        h_ref[...] = h
        o_ref[...] = x + half_swiglu(h)

    @pl.when(j > 0)
    def _():
        o_ref[...] += half_swiglu(h_ref[...])


def _ffn(x, gains, w_in, w_out, layer):
    t, d = x.shape
    f = w_out.shape[1]
    tm, tk = TOKEN_TILE, FF_CHUNK
    nk = f // tk
    assert t % tm == 0 and f % tk == 0
    return pl.pallas_call(
        _ffn_body,
        grid=(t // tm, nk),
        in_specs=[
            pl.BlockSpec((tm, d), lambda i, j: (i, 0)),
            pl.BlockSpec((None, 1, d), lambda i, j: (layer, 0, 0)),
            pl.BlockSpec((None, d, tk), lambda i, j: (layer, 0, j)),
            pl.BlockSpec((None, d, tk), lambda i, j: (layer, 0, j + nk)),
            pl.BlockSpec((None, tk, d), lambda i, j: (layer, j, 0)),
        ],
        out_specs=pl.BlockSpec((tm, d), lambda i, j: (i, 0)),
        out_shape=jax.ShapeDtypeStruct((t, d), F32),
        scratch_shapes=[pltpu.VMEM((tm, d), BF16)],
        compiler_params=_params(
            ("parallel", "arbitrary"),
            4 * _nbytes((tm, d), F32), _nbytes((tm, d), BF16),
            6 * _nbytes((d, tk), BF16), 3 * _nbytes((tm, tk), F32)),
        name="ffn",
    )(x, gains, w_in, w_in, w_out)


def _ple_body(x_ref, p_ref, g_ref, wgate_ref, wp_ref, gf_ref, o_ref, *, final):
    x = x_ref[...]
    h = _rms(x, g_ref[...]).astype(BF16)
    gate = jax.nn.sigmoid(_dot(h, wgate_ref[...]))
    emb = _dot(p_ref[...].astype(BF16), wp_ref[...])
    y = x + gate * emb
    if final:
        y = _rms(y, gf_ref[...])
    o_ref[...] = y


def _ple(x, p, gains, w_gate, w_p, g_final, layer, final):
    t, d = x.shape
    e = p.shape[-1]
    tm = TOKEN_TILE
    return pl.pallas_call(
        functools.partial(_ple_body, final=final),
        grid=(t // tm,),
        in_specs=[
            pl.BlockSpec((tm, d), lambda i: (i, 0)),
            pl.BlockSpec((None, tm, e), lambda i: (layer, i, 0)),
            pl.BlockSpec((None, 1, d), lambda i: (layer, 0, 0)),
            _const_spec((None, d, d), lambda i: (layer, 0, 0)),
            _const_spec((None, e, d), lambda i: (layer, 0, 0)),
            pl.BlockSpec((1, d), lambda i: (0, 0)),
        ],
        out_specs=pl.BlockSpec((tm, d), lambda i: (i, 0)),
        out_shape=jax.ShapeDtypeStruct((t, d), F32),
        compiler_params=_params(
            ("parallel",),
            4 * _nbytes((tm, d), F32), 2 * _nbytes((tm, e), F32),
            _nbytes((d, d), BF16), _nbytes((e, d), BF16), 3 * _nbytes((tm, d), F32)),
        name="ple",
    )(x, p, gains, w_gate, w_p, g_final)


def _channel_dft_table():
    n = A_HEAD_DIM
    heads = V7X_MXU_DIM // n
    idx = np.arange(n)
    ang = 2.0 * np.pi * ((idx[:, None] * idx[None, :]) % n) / n
    eye = np.eye(heads)
    cos = np.kron(eye, np.cos(ang)) / math.sqrt(n)
    sin = np.kron(eye, np.sin(ang)) / math.sqrt(n)
    return jnp.asarray(np.concatenate([cos, sin], axis=1), dtype=BF16)


def _head_mean_table(head_dim):
    heads = V7X_MXU_DIM // head_dim
    m = np.kron(np.eye(heads), np.full((head_dim, head_dim), 1.0 / head_dim))
    return jnp.asarray(m, dtype=BF16)


def _sequence_dft_tables(s, parity):
    base = V7X_MXU_DIM
    half = s // 2
    pos = (2 * jnp.arange(half, dtype=jnp.int32) + parity)[None, :]
    lo = jnp.arange(base, dtype=jnp.int32)[:, None]
    hi = jnp.arange(half // base, dtype=jnp.int32)[:, None] * base
    step = F32(2.0 * math.pi / s)
    ang_lo = ((lo * pos) % s).astype(F32) * step
    ang_hi = ((hi * pos) % s).astype(F32) * step
    scale = F32(1.0 / math.sqrt(s))
    c_lo, s_lo = jnp.cos(ang_lo)[None], jnp.sin(ang_lo)[None]
    c_hi, s_hi = (jnp.cos(ang_hi) * scale)[:, None], (jnp.sin(ang_hi) * scale)[:, None]
    cos = (c_hi * c_lo - s_hi * s_lo).reshape(half, half)
    nsin = -(s_hi * c_lo + c_hi * s_lo).reshape(half, half)
    return cos.astype(BF16), nsin.astype(BF16)


def _mix_in_body(x_ref, g_ref, wa_ref, wu_ref, wv_ref, cs_ref, hm_ref, gv_ref,
                 pe_ref, po_ref, u_ref, vn_ref, p_ref):
    a = wa_ref.shape[1]
    w = V7X_MXU_DIM
    half = pe_ref.shape[0]
    h = _rms(x_ref[...], g_ref[...]).astype(BF16)
    za = _dot(h, wa_ref[...]).astype(BF16)
    lanes = p_ref.shape[-1]
    per_tile = w // lanes
    for t in range(a // w):
        pcs = _dot(za[:, t * w:(t + 1) * w], cs_ref[...])
        for c in range(2 * per_tile):
            n = (c // per_tile) * (a // lanes) + t * per_tile + c % per_tile
            p_ref[n] = pcs[:, c * lanes:(c + 1) * lanes]
    for n in range(p_ref.shape[0]):
        cols = slice(n * lanes, (n + 1) * lanes)
        pe_ref[:, cols] = p_ref[n, pl.ds(0, half, stride=2), :].astype(BF16)
        po_ref[:, cols] = p_ref[n, pl.ds(1, half, stride=2), :].astype(BF16)
    u_ref[...] = jax.nn.gelu(_dot(h, wu_ref[...]))
    v = jax.nn.gelu(_dot(h, wv_ref[...]))
    sq = (v * v).astype(BF16)
    for t in range(v.shape[1] // w):
        cols = slice(t * w, (t + 1) * w)
        ms = _dot(sq[:, cols], hm_ref[...])
        vn_ref[:, cols] = (v[:, cols] * lax.rsqrt(ms + EPS) * gv_ref[:, cols]).astype(BF16)


def _mix_in(x, gains, w_in, cs_tab, hm_tab, g_v, layer, j):
    t, d = x.shape
    a = d // 2
    b = d - a
    tm = MIX_IN_TOKEN_TILE
    assert w_in.shape[2] == a + 2 * b and a == b
    return pl.pallas_call(
        _mix_in_body,
        grid=(t // tm,),
        in_specs=[
            pl.BlockSpec((tm, d), lambda i: (i, 0)),
            pl.BlockSpec((None, 1, d), lambda i: (layer, 0, 0)),
            _const_spec((None, d, a), lambda i: (j, 0, 0)),
            _const_spec((None, d, b), lambda i: (j, 0, 1)),
            _const_spec((None, d, b), lambda i: (j, 0, 2)),
            _const_spec(cs_tab.shape, lambda i: (0, 0)),
            _const_spec(hm_tab.shape, lambda i: (0, 0)),
            pl.BlockSpec((None, 1, b), lambda i: (j, 0, 0)),
        ],
        out_specs=[
            pl.BlockSpec((tm // 2, 2 * a), lambda i: (i, 0)),
            pl.BlockSpec((tm // 2, 2 * a), lambda i: (i, 0)),
            pl.BlockSpec((tm, b), lambda i: (i, 0)),
            pl.BlockSpec((tm, b), lambda i: (i, 0)),
        ],
        out_shape=[
            jax.ShapeDtypeStruct((t // 2, 2 * a), BF16),
            jax.ShapeDtypeStruct((t // 2, 2 * a), BF16),
            jax.ShapeDtypeStruct((t, b), F32),
            jax.ShapeDtypeStruct((t, b), BF16),
        ],
        scratch_shapes=[pltpu.VMEM((2 * a // V7X_LANES, tm, V7X_LANES), F32)],
        compiler_params=_params(
            ("parallel",),
            2 * _nbytes((tm, d), F32), 3 * _nbytes((d, a), BF16),
            2 * _nbytes((tm, 2 * a), BF16), 2 * _nbytes((tm, b), F32),
            2 * _nbytes((tm, b), BF16), _nbytes((tm, 2 * a), F32),
            4 * _nbytes((tm, a), F32)),
        name="mix_in",
    )(x, gains, w_in, w_in, w_in, cs_tab, hm_tab, g_v)


def _seq_dft_body(ce_ref, se_ref, co_ref, so_ref, pe_ref, po_ref, o_ref, *acc, k_steps):
    a = o_ref.shape[-1]
    kk = pl.program_id(2)
    if k_steps > 1:
        acc_e, acc_o = acc

        @pl.when(kk == 0)
        def _():
            acc_e[...] = jnp.zeros_like(acc_e)
            acc_o[...] = jnp.zeros_like(acc_o)

    even = _dot(ce_ref[...], pe_ref[:, :a]) + _dot(se_ref[...], pe_ref[:, a:])
    odd = _dot(co_ref[...], po_ref[:, :a]) + _dot(so_ref[...], po_ref[:, a:])
    if k_steps == 1:
        o_ref[0] = (even + odd).astype(BF16)
        o_ref[1] = (even - odd).astype(BF16)
        return
    acc_e[...] += even
    acc_o[...] += odd

    @pl.when(kk == k_steps - 1)
    def _():
        o_ref[0] = (acc_e[...] + acc_o[...]).astype(BF16)
        o_ref[1] = (acc_e[...] - acc_o[...]).astype(BF16)


def _seq_dft(pe, po, tabs_even, tabs_odd):
    bsz, half, a2 = pe.shape
    a = a2 // 2
    tr = min(DFT_ROW_TILE, half)
    tc = min(DFT_K_CHUNK, half)
    k_steps = half // tc
    tab_spec = pl.BlockSpec((tr, tc), lambda b, r, k: (r, k))
    p_spec = pl.BlockSpec((None, tc, a2), lambda b, r, k: (b, k, 0))
    return pl.pallas_call(
        functools.partial(_seq_dft_body, k_steps=k_steps),
        grid=(bsz, half // tr, k_steps),
        in_specs=[tab_spec, tab_spec, tab_spec, tab_spec, p_spec, p_spec],
        out_specs=pl.BlockSpec((None, 2, tr, a), lambda b, r, k: (b, 0, r, 0)),
        out_shape=jax.ShapeDtypeStruct((bsz, 2, half, a), BF16),
        scratch_shapes=[pltpu.VMEM((tr, a), F32)] * 2 if k_steps > 1 else [],
        compiler_params=_params(
            ("parallel", "parallel", "arbitrary"),
            8 * _nbytes((tr, tc), BF16), 4 * _nbytes((tc, a2), BF16),
            2 * _nbytes((2, tr, a), BF16), 4 * _nbytes((tr, a), F32)),
        name="seq_dft",
    )(*tabs_even, *tabs_odd, pe, po)


def _mix_out_body(x_ref, ya_ref, u_ref, vn_ref, ws_ref, bias_ref, wo_ref, o_ref, y_ref,
                  *, head_dim):
    tm, a = ya_ref.shape
    chunk = ws_ref.shape[1]
    w = V7X_MXU_DIM
    heads = w // head_dim
    lane_head = lax.broadcasted_iota(jnp.int32, (chunk, w), 1) // head_dim
    y_ref[:, :a] = ya_ref[...]
    for c in range(tm // chunk):
        rows = slice(c * chunk, (c + 1) * chunk)
        for q in range(vn_ref.shape[1] // w):
            cols = slice(q * w, (q + 1) * w)
            stacked = _dot(ws_ref[q * heads * chunk:(q + 1) * heads * chunk, :],
                           vn_ref[rows, cols])
            mixed = stacked[:chunk]
            for r in range(1, heads):
                mixed = jnp.where(lane_head == r, stacked[r * chunk:(r + 1) * chunk], mixed)
            yb = u_ref[rows, cols] * (mixed + bias_ref[:, cols])
            y_ref[rows, a + q * w:a + (q + 1) * w] = yb.astype(BF16)
    o_ref[...] = x_ref[...] + _dot(y_ref[...], wo_ref[...])


def _mix_out(x, ya, u, vn, w_s, bias, w_out, j, head_dim):
    t, d = x.shape
    a = ya.shape[1]
    b = u.shape[1]
    chunk = w_s.shape[2]
    tm = TOKEN_TILE
    assert tm % chunk == 0 and b % V7X_MXU_DIM == 0 and V7X_MXU_DIM % head_dim == 0
    return pl.pallas_call(
        functools.partial(_mix_out_body, head_dim=head_dim),
        grid=(t // tm,),
        in_specs=[
            pl.BlockSpec((tm, d), lambda i: (i, 0)),
            pl.BlockSpec((tm, a), lambda i: (i, 0)),
            pl.BlockSpec((tm, b), lambda i: (i, 0)),
            pl.BlockSpec((tm, b), lambda i: (i, 0)),
            _const_spec((None,) + w_s.shape[1:], lambda i: (j, 0, 0)),
            _const_spec((None,) + bias.shape[1:], lambda i: (j, 0, 0)),
            _const_spec((None, a + b, d), lambda i: (j, 0, 0)),
        ],
        out_specs=pl.BlockSpec((tm, d), lambda i: (i, 0)),
        out_shape=jax.ShapeDtypeStruct((t, d), F32),
        scratch_shapes=[pltpu.VMEM((tm, a + b), BF16)],
        compiler_params=_params(
            ("parallel",),
            4 * _nbytes((tm, d), F32), 2 * _nbytes((tm, a), BF16),
            2 * _nbytes((tm, b), F32), 2 * _nbytes((tm, b), BF16),
            _nbytes(w_s.shape[1:], BF16), _nbytes(bias.shape[1:], F32),
            _nbytes((a + b, d), BF16), _nbytes((tm, a + b), BF16),
            2 * _nbytes((tm, d), F32)),
        name="mix_out",
    )(x, ya, u, vn, w_s, bias, w_out)


def _conv_in_body(x_ref, g_ref, wb_ref, wc_ref, wx_ref, gb_ref, z_ref, h_ref):
    def project(h):
        gb_ref[...] = _dot(h, wb_ref[...]).astype(BF16)
        z_ref[...] = (_dot(h, wc_ref[...]) * _dot(h, wx_ref[...])).astype(BF16)

    @pl.when(pl.program_id(1) == 0)
    def _():
        h = _rms(x_ref[...], g_ref[...]).astype(BF16)
        h_ref[...] = h
        project(h)

    @pl.when(pl.program_id(1) > 0)
    def _():
        project(h_ref[...])


def _conv_in(x, gains, w_in, layer, j):
    t, d = x.shape
    c = w_in.shape[2] // 3
    tm, tn = TOKEN_TILE, CONV_IN_CHUNK
    nn = c // tn
    return pl.pallas_call(
        _conv_in_body,
        grid=(t // tm, nn),
        in_specs=[
            pl.BlockSpec((tm, d), lambda i, n: (i, 0)),
            pl.BlockSpec((None, 1, d), lambda i, n: (layer, 0, 0)),
            pl.BlockSpec((None, d, tn), lambda i, n: (j, 0, n)),
            pl.BlockSpec((None, d, tn), lambda i, n: (j, 0, n + nn)),
            pl.BlockSpec((None, d, tn), lambda i, n: (j, 0, n + 2 * nn)),
        ],
        out_specs=[
            pl.BlockSpec((tm, tn), lambda i, n: (i, n)),
            pl.BlockSpec((tm, tn), lambda i, n: (i, n)),
        ],
        out_shape=[jax.ShapeDtypeStruct((t, c), BF16), jax.ShapeDtypeStruct((t, c), BF16)],
        scratch_shapes=[pltpu.VMEM((tm, d), BF16)],
        compiler_params=_params(
            ("parallel", "arbitrary"),
            2 * _nbytes((tm, d), F32), _nbytes((tm, d), BF16),
            6 * _nbytes((d, tn), BF16), 4 * _nbytes((tm, tn), BF16),
            3 * _nbytes((tm, tn), F32)),
        name="conv_in",
    )(x, gains, w_in, w_in, w_in)


def _conv_out_body(x_ref, z_ref, zp_ref, zn_ref, gb_ref, wc_ref, bc_ref, wo_ref, o_ref,
                   *, tiles_per_seq):
    tm = z_ref.shape[0]
    i = pl.program_id(0) % tiles_per_seq
    z = z_ref[...].astype(F32)
    halo = zp_ref.shape[0]
    before = jnp.where(i == 0, 0.0, zp_ref[...].astype(F32)[halo - 1:halo])
    after = jnp.where(i == tiles_per_seq - 1, 0.0, zn_ref[...].astype(F32)[0:1])
    row = lax.broadcasted_iota(jnp.int32, z.shape, 0)
    z_prev = jnp.where(row == 0, before, pltpu.roll(z, 1, axis=0))
    z_next = jnp.where(row == tm - 1, after, pltpu.roll(z, tm - 1, axis=0))
    conv = z_prev * wc_ref[0:1] + z * wc_ref[1:2] + z_next * wc_ref[2:3] + bc_ref[...]
    y = (gb_ref[...].astype(F32) * conv).astype(BF16)
    o_ref[...] = x_ref[...] + _dot(y, wo_ref[...])


def _conv_out(x, z, gb, w_conv, b_conv, w_out, j, seq):
    t, d = x.shape
    c = z.shape[1]
    tm = TOKEN_TILE
    halo = 16
    assert seq % tm == 0 and tm % halo == 0
    per = tm // halo
    last = t // halo - 1
    return pl.pallas_call(
        functools.partial(_conv_out_body, tiles_per_seq=seq // tm),
        grid=(t // tm,),
        in_specs=[
            pl.BlockSpec((tm, d), lambda i: (i, 0)),
            pl.BlockSpec((tm, c), lambda i: (i, 0)),
            pl.BlockSpec((halo, c), lambda i: (jnp.maximum(i * per - 1, 0), 0)),
            pl.BlockSpec((halo, c), lambda i: (jnp.minimum((i + 1) * per, last), 0)),
            pl.BlockSpec((tm, c), lambda i: (i, 0)),
            pl.BlockSpec((None,) + w_conv.shape[1:], lambda i: (j, 0, 0)),
            pl.BlockSpec((None, 1, c), lambda i: (j, 0, 0)),
            _const_spec((None, c, d), lambda i: (j, 0, 0)),
        ],
        out_specs=pl.BlockSpec((tm, d), lambda i: (i, 0)),
        out_shape=jax.ShapeDtypeStruct((t, d), F32),
        compiler_params=_params(
            ("parallel",),
            4 * _nbytes((tm, d), F32), 4 * _nbytes((tm, c), BF16),
            _nbytes((c, d), BF16), 5 * _nbytes((tm, c), F32)),
        name="conv_out",
    )(x, z, z, z, gb, w_conv, b_conv, w_out)


def _trunk(x, p, w, tabs):
    bsz, seq, d = x.shape
    depth = p.shape[0]
    t = bsz * seq
    x = x.reshape(t, d)
    p = p.reshape(depth, t, p.shape[-1])
    tabs_even, tabs_odd = _sequence_dft_tables(seq, 0), _sequence_dft_tables(seq, 1)
    head_dim = tabs["head_dim"]
    for i in range(depth):
        x = _ffn(x, w["g_ffn1"], w["w_ffn1_in"], w["w_ffn1_out"], i)
        j = i // 2
        if i % 2 == 0:
            pe, po, u, vn = _mix_in(x, w["g_mix"], w["w_in_ab"], tabs["cs"], tabs["hm"],
                                    w["g_v"], i, j)
            ya = _seq_dft(pe.reshape(bsz, seq // 2, pe.shape[-1]),
                          po.reshape(bsz, seq // 2, po.shape[-1]), tabs_even, tabs_odd)
            x = _mix_out(x, ya.reshape(t, ya.shape[-1]), u, vn, w["w_s"], w["bias_s"],
                         w["w_out_ab"], j, head_dim)
        else:
            gb, z = _conv_in(x, w["g_mix"], w["w_in_c"], i, j)
            x = _conv_out(x, z, gb, w["w_conv"], w["b_conv"], w["w_out_c"], j, seq)
        x = _ffn(x, w["g_ffn2"], w["w_ffn2_in"], w["w_ffn2_out"], i)
        x = _ple(x, p, w["g_ple"], w["w_ple_gate"], w["w_ple"], w["g_final"], i,
                 final=(i == depth - 1))
    return x.reshape(bsz, seq, d)


def kernel(x_prompt, x_sample, p_prompt, p_sample, g_ffn1, w_ffn1_in, w_ffn1_out, g_mix, w_in_ab, g_v, w_s, b_s, w_out_ab, w_in_c, w_conv, b_conv, w_out_c, g_ffn2, w_ffn2_in, w_ffn2_out, g_ple, w_ple_gate, w_ple, g_final):
    n_even, n_heads, head_dim = g_v.shape
    chunk = w_s.shape[-1]
    gain = lambda g: g[:, None, :]
    w = {
        "g_ffn1": gain(g_ffn1), "g_mix": gain(g_mix), "g_ffn2": gain(g_ffn2),
        "g_ple": gain(g_ple), "g_final": g_final[None, :],
        "g_v": g_v.reshape(n_even, 1, n_heads * head_dim),
        "w_ffn1_in": w_ffn1_in.astype(BF16), "w_ffn1_out": w_ffn1_out.astype(BF16),
        "w_ffn2_in": w_ffn2_in.astype(BF16), "w_ffn2_out": w_ffn2_out.astype(BF16),
        "w_in_ab": w_in_ab.astype(BF16), "w_out_ab": w_out_ab.astype(BF16),
        "w_s": w_s.astype(BF16).reshape(n_even, n_heads * chunk, chunk),
        "bias_s": jnp.repeat(jnp.swapaxes(b_s, 1, 2), head_dim, axis=2),
        "w_in_c": w_in_c.astype(BF16), "w_out_c": w_out_c.astype(BF16),
        "w_conv": w_conv, "b_conv": b_conv[:, None, :],
        "w_ple_gate": w_ple_gate.astype(BF16), "w_ple": w_ple.astype(BF16),
    }
    tabs = {"cs": _channel_dft_table(), "hm": _head_mean_table(head_dim), "head_dim": head_dim}
    return (_trunk(x_prompt, p_prompt, w, tabs), _trunk(x_sample, p_sample, w, tabs))
```

```python
import functools
import math

import numpy as np
import jax
import jax.numpy as jnp
from jax import lax
from jax.experimental import pallas as pl
from jax.experimental.pallas import tpu as pltpu

EPS = 1e-6
A_HEAD_DIM = 64
BF16 = jnp.bfloat16
F32 = jnp.float32

V7X_VMEM_BYTES = 64 * 1024 * 1024
V7X_MXU_DIM = 256
V7X_LANES = 128
BF16_SUBLANES = 16
VMEM_RESERVE_BYTES = 6 * 1024 * 1024

TOKEN_TILE = 1024
MIX_IN_TOKEN_TILE = 512
FF_CHUNK = 512
CONV_IN_CHUNK = 512
DFT_ROW_TILE = 1024
DFT_K_CHUNK = 1024


def _vmem_limit(*nbytes):
    return int(min(sum(nbytes) + VMEM_RESERVE_BYTES, V7X_VMEM_BYTES - 2 * 1024 * 1024))


def _nbytes(shape, dtype):
    return int(np.prod(shape)) * jnp.dtype(dtype).itemsize


def _params(semantics, *nbytes):
    return pltpu.CompilerParams(dimension_semantics=semantics,
                                vmem_limit_bytes=_vmem_limit(*nbytes))


def _const_spec(block, index_map):
    return pl.BlockSpec(block, index_map, pipeline_mode=pl.Buffered(1))


def _rms(x, g):
    ms = jnp.mean(x * x, axis=-1, keepdims=True)
    return x * lax.rsqrt(ms + EPS) * g


def _dot(a, b):
    return jnp.dot(a, b, preferred_element_type=F32)


def _ffn_body(x_ref, g_ref, wg_ref, wu_ref, wo_ref, *rest, n_cast):
    cast_in, rest = rest[:n_cast], rest[n_cast:]
    o_ref, cast_out, h_ref = rest[0], rest[1:1 + n_cast], rest[1 + n_cast]
    j = pl.program_id(1)

    def half_swiglu(h):
        _cast_blocks(cast_in, cast_out)
        gate = _dot(h, wg_ref[...])
        up = _dot(h, wu_ref[...])
        act = (gate * jax.nn.sigmoid(gate) * up).astype(BF16)
        return 0.5 * _dot(act, wo_ref[...])

    @pl.when(j == 0)
    def _():
        x = x_ref[...]
        h = _rms(x, g_ref[...]).astype(BF16)
        h_ref[...] = h
        o_ref[...] = x + half_swiglu(h)

    @pl.when(j > 0)
    def _():
        o_ref[...] += half_swiglu(h_ref[...])


def _cast_rows(rows, steps):
    per = -(-rows // steps)
    per = -(-per // BF16_SUBLANES) * BF16_SUBLANES
    while per < rows and rows % per:
        per += BF16_SUBLANES
    return per if rows % per == 0 else None


class _CastJobs:
    def __init__(self, sources, steps, step_of):
        self.in_specs, self.out_specs, self.out_shape = [], [], []
        self.operands, self.nbytes = [], []
        for src, layer in sources:
            rows, cols = src.shape[1:]
            per = _cast_rows(rows, steps)
            assert per is not None
            last = rows // per - 1
            block = lambda *idx, last=last: jnp.minimum(step_of(*idx), last)
            self.in_specs.append(pl.BlockSpec(
                (None, per, cols), lambda *idx, block=block, layer=layer: (layer, block(*idx), 0)))
            self.out_specs.append(pl.BlockSpec(
                (per, cols), lambda *idx, block=block: (block(*idx), 0)))
            self.out_shape.append(jax.ShapeDtypeStruct((rows, cols), BF16))
            self.operands.append(src)
            self.nbytes += [2 * _nbytes((per, cols), F32), 2 * _nbytes((per, cols), BF16)]

    def __len__(self):
        return len(self.operands)


def _cast_blocks(cast_in, cast_out):
    for src, dst in zip(cast_in, cast_out):
        dst[...] = src[...].astype(BF16)


def _ffn(x, gains, w_in, w_out, layer, cast_sources=()):
    t, d = x.shape
    f = w_out.shape[0]
    tm, tk = TOKEN_TILE, FF_CHUNK
    nk = f // tk
    assert t % tm == 0 and f % tk == 0
    jobs = _CastJobs(cast_sources, (t // tm) * nk, lambda i, j: i * nk + j)
    outs = pl.pallas_call(
        functools.partial(_ffn_body, n_cast=len(jobs)),
        grid=(t // tm, nk),
        in_specs=[
            pl.BlockSpec((tm, d), lambda i, j: (i, 0)),
            pl.BlockSpec((None, 1, d), lambda i, j: (layer, 0, 0)),
            pl.BlockSpec((d, tk), lambda i, j: (0, j)),
            pl.BlockSpec((d, tk), lambda i, j: (0, j + nk)),
            pl.BlockSpec((tk, d), lambda i, j: (j, 0)),
        ] + jobs.in_specs,
        out_specs=[pl.BlockSpec((tm, d), lambda i, j: (i, 0))] + jobs.out_specs,
        out_shape=[jax.ShapeDtypeStruct((t, d), F32)] + jobs.out_shape,
        scratch_shapes=[pltpu.VMEM((tm, d), BF16)],
        compiler_params=_params(
            ("arbitrary", "arbitrary"),
            4 * _nbytes((tm, d), F32), _nbytes((tm, d), BF16),
            6 * _nbytes((d, tk), BF16), 3 * _nbytes((tm, tk), F32), *jobs.nbytes),
        name="ffn",
    )(x, gains, w_in, w_in, w_out, *jobs.operands)
    return outs[0], tuple(outs[1:])


def _ple_body(x_ref, p_ref, g_ref, wgate_ref, wp_ref, gf_ref, o_ref, *, final):
    x = x_ref[...]
    h = _rms(x, g_ref[...]).astype(BF16)
    gate = jax.nn.sigmoid(_dot(h, wgate_ref[...]))
    emb = _dot(p_ref[...].astype(BF16), wp_ref[...])
    y = x + gate * emb
    if final:
        y = _rms(y, gf_ref[...])
    o_ref[...] = y


def _ple(x, p, gains, w_gate, w_p, g_final, layer, final):
    t, d = x.shape
    e = p.shape[-1]
    tm = TOKEN_TILE
    return pl.pallas_call(
        functools.partial(_ple_body, final=final),
        grid=(t // tm,),
        in_specs=[
            pl.BlockSpec((tm, d), lambda i: (i, 0)),
            pl.BlockSpec((None, tm, e), lambda i: (layer, i, 0)),
            pl.BlockSpec((None, 1, d), lambda i: (layer, 0, 0)),
            _const_spec((None, d, d), lambda i: (layer, 0, 0)),
            _const_spec((None, e, d), lambda i: (layer, 0, 0)),
            pl.BlockSpec((1, d), lambda i: (0, 0)),
        ],
        out_specs=pl.BlockSpec((tm, d), lambda i: (i, 0)),
        out_shape=jax.ShapeDtypeStruct((t, d), F32),
        compiler_params=_params(
            ("parallel",),
            4 * _nbytes((tm, d), F32), 2 * _nbytes((tm, e), F32),
            _nbytes((d, d), BF16), _nbytes((e, d), BF16), 3 * _nbytes((tm, d), F32)),
        name="ple",
    )(x, p, gains, w_gate, w_p, g_final)


def _channel_dft_table():
    n = A_HEAD_DIM
    heads = V7X_MXU_DIM // n
    idx = np.arange(n)
    ang = 2.0 * np.pi * ((idx[:, None] * idx[None, :]) % n) / n
    eye = np.eye(heads)
    cos = np.kron(eye, np.cos(ang)) / math.sqrt(n)
    sin = np.kron(eye, np.sin(ang)) / math.sqrt(n)
    return jnp.asarray(np.concatenate([cos, sin], axis=1), dtype=BF16)


def _head_mean_table(head_dim):
    heads = V7X_MXU_DIM // head_dim
    m = np.kron(np.eye(heads), np.full((head_dim, head_dim), 1.0 / head_dim))
    return jnp.asarray(m, dtype=BF16)


def _sequence_dft_tables(s, parity):
    base = V7X_MXU_DIM
    half = s // 2
    pos = (2 * jnp.arange(half, dtype=jnp.int32) + parity)[None, :]
    lo = jnp.arange(base, dtype=jnp.int32)[:, None]
    hi = jnp.arange(half // base, dtype=jnp.int32)[:, None] * base
    step = F32(2.0 * math.pi / s)
    ang_lo = ((lo * pos) % s).astype(F32) * step
    ang_hi = ((hi * pos) % s).astype(F32) * step
    scale = F32(1.0 / math.sqrt(s))
    c_lo, s_lo = jnp.cos(ang_lo)[None], jnp.sin(ang_lo)[None]
    c_hi, s_hi = (jnp.cos(ang_hi) * scale)[:, None], (jnp.sin(ang_hi) * scale)[:, None]
    cos = (c_hi * c_lo - s_hi * s_lo).reshape(half, half)
    nsin = -(s_hi * c_lo + c_hi * s_lo).reshape(half, half)
    return cos.astype(BF16), nsin.astype(BF16)


def _mix_in_body(x_ref, g_ref, wa_ref, wu_ref, wv_ref, cs_ref, hm_ref, gv_ref,
                 pe_ref, po_ref, u_ref, vn_ref, p_ref):
    a = wa_ref.shape[1]
    w = V7X_MXU_DIM
    half = pe_ref.shape[0]
    h = _rms(x_ref[...], g_ref[...]).astype(BF16)
    za = _dot(h, wa_ref[...]).astype(BF16)
    lanes = p_ref.shape[-1]
    per_tile = w // lanes
    for t in range(a // w):
        pcs = _dot(za[:, t * w:(t + 1) * w], cs_ref[...])
        for c in range(2 * per_tile):
            n = (c // per_tile) * (a // lanes) + t * per_tile + c % per_tile
            p_ref[n] = pcs[:, c * lanes:(c + 1) * lanes]
    for n in range(p_ref.shape[0]):
        cols = slice(n * lanes, (n + 1) * lanes)
        pe_ref[:, cols] = p_ref[n, pl.ds(0, half, stride=2), :].astype(BF16)
        po_ref[:, cols] = p_ref[n, pl.ds(1, half, stride=2), :].astype(BF16)
    u_ref[...] = jax.nn.gelu(_dot(h, wu_ref[...]))
    v = jax.nn.gelu(_dot(h, wv_ref[...]))
    sq = (v * v).astype(BF16)
    for t in range(v.shape[1] // w):
        cols = slice(t * w, (t + 1) * w)
        ms = _dot(sq[:, cols], hm_ref[...])
        vn_ref[:, cols] = (v[:, cols] * lax.rsqrt(ms + EPS) * gv_ref[:, cols]).astype(BF16)


def _mix_in(x, gains, w_in, cs_tab, hm_tab, g_v, layer, j):
    t, d = x.shape
    a = d // 2
    b = d - a
    tm = MIX_IN_TOKEN_TILE
    assert w_in.shape[2] == a + 2 * b and a == b
    return pl.pallas_call(
        _mix_in_body,
        grid=(t // tm,),
        in_specs=[
            pl.BlockSpec((tm, d), lambda i: (i, 0)),
            pl.BlockSpec((None, 1, d), lambda i: (layer, 0, 0)),
            _const_spec((None, d, a), lambda i: (j, 0, 0)),
            _const_spec((None, d, b), lambda i: (j, 0, 1)),
            _const_spec((None, d, b), lambda i: (j, 0, 2)),
            _const_spec(cs_tab.shape, lambda i: (0, 0)),
            _const_spec(hm_tab.shape, lambda i: (0, 0)),
            pl.BlockSpec((None, 1, b), lambda i: (j, 0, 0)),
        ],
        out_specs=[
            pl.BlockSpec((tm // 2, 2 * a), lambda i: (i, 0)),
            pl.BlockSpec((tm // 2, 2 * a), lambda i: (i, 0)),
            pl.BlockSpec((tm, b), lambda i: (i, 0)),
            pl.BlockSpec((tm, b), lambda i: (i, 0)),
        ],
        out_shape=[
            jax.ShapeDtypeStruct((t // 2, 2 * a), BF16),
            jax.ShapeDtypeStruct((t // 2, 2 * a), BF16),
            jax.ShapeDtypeStruct((t, b), F32),
            jax.ShapeDtypeStruct((t, b), BF16),
        ],
        scratch_shapes=[pltpu.VMEM((2 * a // V7X_LANES, tm, V7X_LANES), F32)],
        compiler_params=_params(
            ("parallel",),
            2 * _nbytes((tm, d), F32), 3 * _nbytes((d, a), BF16),
            2 * _nbytes((tm, 2 * a), BF16), 2 * _nbytes((tm, b), F32),
            2 * _nbytes((tm, b), BF16), _nbytes((tm, 2 * a), F32),
            4 * _nbytes((tm, a), F32)),
        name="mix_in",
    )(x, gains, w_in, w_in, w_in, cs_tab, hm_tab, g_v)


def _seq_dft_body(ce_ref, se_ref, co_ref, so_ref, pe_ref, po_ref, o_ref, *acc, k_steps):
    a = o_ref.shape[-1]
    kk = pl.program_id(2)
    if k_steps > 1:
        acc_e, acc_o = acc

        @pl.when(kk == 0)
        def _():
            acc_e[...] = jnp.zeros_like(acc_e)
            acc_o[...] = jnp.zeros_like(acc_o)

    even = _dot(ce_ref[...], pe_ref[:, :a]) + _dot(se_ref[...], pe_ref[:, a:])
    odd = _dot(co_ref[...], po_ref[:, :a]) + _dot(so_ref[...], po_ref[:, a:])
    if k_steps == 1:
        o_ref[0] = (even + odd).astype(BF16)
        o_ref[1] = (even - odd).astype(BF16)
        return
    acc_e[...] += even
    acc_o[...] += odd

    @pl.when(kk == k_steps - 1)
    def _():
        o_ref[0] = (acc_e[...] + acc_o[...]).astype(BF16)
        o_ref[1] = (acc_e[...] - acc_o[...]).astype(BF16)


def _seq_dft(pe, po, tabs_even, tabs_odd):
    bsz, half, a2 = pe.shape
    a = a2 // 2
    tr = min(DFT_ROW_TILE, half)
    tc = min(DFT_K_CHUNK, half)
    k_steps = half // tc
    tab_spec = pl.BlockSpec((tr, tc), lambda b, r, k: (r, k))
    p_spec = pl.BlockSpec((None, tc, a2), lambda b, r, k: (b, k, 0))
    return pl.pallas_call(
        functools.partial(_seq_dft_body, k_steps=k_steps),
        grid=(bsz, half // tr, k_steps),
        in_specs=[tab_spec, tab_spec, tab_spec, tab_spec, p_spec, p_spec],
        out_specs=pl.BlockSpec((None, 2, tr, a), lambda b, r, k: (b, 0, r, 0)),
        out_shape=jax.ShapeDtypeStruct((bsz, 2, half, a), BF16),
        scratch_shapes=[pltpu.VMEM((tr, a), F32)] * 2 if k_steps > 1 else [],
        compiler_params=_params(
            ("parallel", "parallel", "arbitrary"),
            8 * _nbytes((tr, tc), BF16), 4 * _nbytes((tc, a2), BF16),
            2 * _nbytes((2, tr, a), BF16), 4 * _nbytes((tr, a), F32)),
        name="seq_dft",
    )(*tabs_even, *tabs_odd, pe, po)


def _mix_out_body(x_ref, ya_ref, u_ref, vn_ref, ws_ref, bias_ref, wo_ref, o_ref, y_ref,
                  *, head_dim):
    tm, a = ya_ref.shape
    chunk = ws_ref.shape[1]
    w = V7X_MXU_DIM
    heads = w // head_dim
    lane_head = lax.broadcasted_iota(jnp.int32, (chunk, w), 1) // head_dim
    y_ref[:, :a] = ya_ref[...]
    for c in range(tm // chunk):
        rows = slice(c * chunk, (c + 1) * chunk)
        for q in range(vn_ref.shape[1] // w):
            cols = slice(q * w, (q + 1) * w)
            stacked = _dot(ws_ref[q * heads * chunk:(q + 1) * heads * chunk, :],
                           vn_ref[rows, cols])
            mixed = stacked[:chunk]
            for r in range(1, heads):
                mixed = jnp.where(lane_head == r, stacked[r * chunk:(r + 1) * chunk], mixed)
            yb = u_ref[rows, cols] * (mixed + bias_ref[:, cols])
            y_ref[rows, a + q * w:a + (q + 1) * w] = yb.astype(BF16)
    o_ref[...] = x_ref[...] + _dot(y_ref[...], wo_ref[...])


def _mix_out(x, ya, u, vn, w_s, bias, w_out, j, head_dim):
    t, d = x.shape
    a = ya.shape[1]
    b = u.shape[1]
    chunk = w_s.shape[2]
    tm = TOKEN_TILE
    assert tm % chunk == 0 and b % V7X_MXU_DIM == 0 and V7X_MXU_DIM % head_dim == 0
    return pl.pallas_call(
        functools.partial(_mix_out_body, head_dim=head_dim),
        grid=(t // tm,),
        in_specs=[
            pl.BlockSpec((tm, d), lambda i: (i, 0)),
            pl.BlockSpec((tm, a), lambda i: (i, 0)),
            pl.BlockSpec((tm, b), lambda i: (i, 0)),
            pl.BlockSpec((tm, b), lambda i: (i, 0)),
            _const_spec((None,) + w_s.shape[1:], lambda i: (j, 0, 0)),
            _const_spec((None,) + bias.shape[1:], lambda i: (j, 0, 0)),
            _const_spec((None, a + b, d), lambda i: (j, 0, 0)),
        ],
        out_specs=pl.BlockSpec((tm, d), lambda i: (i, 0)),
        out_shape=jax.ShapeDtypeStruct((t, d), F32),
        scratch_shapes=[pltpu.VMEM((tm, a + b), BF16)],
        compiler_params=_params(
            ("parallel",),
            4 * _nbytes((tm, d), F32), 2 * _nbytes((tm, a), BF16),
            2 * _nbytes((tm, b), F32), 2 * _nbytes((tm, b), BF16),
            _nbytes(w_s.shape[1:], BF16), _nbytes(bias.shape[1:], F32),
            _nbytes((a + b, d), BF16), _nbytes((tm, a + b), BF16),
            2 * _nbytes((tm, d), F32)),
        name="mix_out",
    )(x, ya, u, vn, w_s, bias, w_out)


def _conv_in_body(x_ref, g_ref, wb_ref, wc_ref, wx_ref, *rest, n_cast):
    cast_in, rest = rest[:n_cast], rest[n_cast:]
    gb_ref, z_ref, cast_out, h_ref = rest[0], rest[1], rest[2:2 + n_cast], rest[2 + n_cast]

    def project(h):
        _cast_blocks(cast_in, cast_out)
        gb_ref[...] = _dot(h, wb_ref[...]).astype(BF16)
        z_ref[...] = (_dot(h, wc_ref[...]) * _dot(h, wx_ref[...])).astype(BF16)

    @pl.when(pl.program_id(1) == 0)
    def _():
        h = _rms(x_ref[...], g_ref[...]).astype(BF16)
        h_ref[...] = h
        project(h)

    @pl.when(pl.program_id(1) > 0)
    def _():
        project(h_ref[...])


def _conv_in(x, gains, w_in, layer, j, cast_sources=()):
    t, d = x.shape
    c = w_in.shape[2] // 3
    tm, tn = TOKEN_TILE, CONV_IN_CHUNK
    nn = c // tn
    jobs = _CastJobs(cast_sources, (t // tm) * nn, lambda i, n: i * nn + n)
    outs = pl.pallas_call(
        functools.partial(_conv_in_body, n_cast=len(jobs)),
        grid=(t // tm, nn),
        in_specs=[
            pl.BlockSpec((tm, d), lambda i, n: (i, 0)),
            pl.BlockSpec((None, 1, d), lambda i, n: (layer, 0, 0)),
            pl.BlockSpec((None, d, tn), lambda i, n: (j, 0, n)),
            pl.BlockSpec((None, d, tn), lambda i, n: (j, 0, n + nn)),
            pl.BlockSpec((None, d, tn), lambda i, n: (j, 0, n + 2 * nn)),
        ] + jobs.in_specs,
        out_specs=[
            pl.BlockSpec((tm, tn), lambda i, n: (i, n)),
            pl.BlockSpec((tm, tn), lambda i, n: (i, n)),
        ] + jobs.out_specs,
        out_shape=[jax.ShapeDtypeStruct((t, c), BF16),
                   jax.ShapeDtypeStruct((t, c), BF16)] + jobs.out_shape,
        scratch_shapes=[pltpu.VMEM((tm, d), BF16)],
        compiler_params=_params(
            ("arbitrary", "arbitrary"),
            2 * _nbytes((tm, d), F32), _nbytes((tm, d), BF16),
            6 * _nbytes((d, tn), BF16), 4 * _nbytes((tm, tn), BF16),
            3 * _nbytes((tm, tn), F32), *jobs.nbytes),
        name="conv_in",
    )(x, gains, w_in, w_in, w_in, *jobs.operands)
    return outs[0], outs[1], tuple(outs[2:])


def _conv_out_body(x_ref, z_ref, zp_ref, zn_ref, gb_ref, wc_ref, bc_ref, wo_ref, o_ref,
                   *, tiles_per_seq):
    tm = z_ref.shape[0]
    i = pl.program_id(0) % tiles_per_seq
    z = z_ref[...].astype(F32)
    halo = zp_ref.shape[0]
    before = jnp.where(i == 0, 0.0, zp_ref[...].astype(F32)[halo - 1:halo])
    after = jnp.where(i == tiles_per_seq - 1, 0.0, zn_ref[...].astype(F32)[0:1])
    row = lax.broadcasted_iota(jnp.int32, z.shape, 0)
    z_prev = jnp.where(row == 0, before, pltpu.roll(z, 1, axis=0))
    z_next = jnp.where(row == tm - 1, after, pltpu.roll(z, tm - 1, axis=0))
    conv = z_prev * wc_ref[0:1] + z * wc_ref[1:2] + z_next * wc_ref[2:3] + bc_ref[...]
    y = (gb_ref[...].astype(F32) * conv).astype(BF16)
    o_ref[...] = x_ref[...] + _dot(y, wo_ref[...])


def _conv_out(x, z, gb, w_conv, b_conv, w_out, j, seq):
    t, d = x.shape
    c = z.shape[1]
    tm = TOKEN_TILE
    halo = 16
    assert seq % tm == 0 and tm % halo == 0
    per = tm // halo
    last = t // halo - 1
    return pl.pallas_call(
        functools.partial(_conv_out_body, tiles_per_seq=seq // tm),
        grid=(t // tm,),
        in_specs=[
            pl.BlockSpec((tm, d), lambda i: (i, 0)),
            pl.BlockSpec((tm, c), lambda i: (i, 0)),
            pl.BlockSpec((halo, c), lambda i: (jnp.maximum(i * per - 1, 0), 0)),
            pl.BlockSpec((halo, c), lambda i: (jnp.minimum((i + 1) * per, last), 0)),
            pl.BlockSpec((tm, c), lambda i: (i, 0)),
            pl.BlockSpec((None,) + w_conv.shape[1:], lambda i: (j, 0, 0)),
            pl.BlockSpec((None, 1, c), lambda i: (j, 0, 0)),
            _const_spec((None, c, d), lambda i: (j, 0, 0)),
        ],
        out_specs=pl.BlockSpec((tm, d), lambda i: (i, 0)),
        out_shape=jax.ShapeDtypeStruct((t, d), F32),
        compiler_params=_params(
            ("parallel",),
            4 * _nbytes((tm, d), F32), 4 * _nbytes((tm, c), BF16),
            _nbytes((c, d), BF16), 5 * _nbytes((tm, c), F32)),
        name="conv_out",
    )(x, z, z, z, gb, w_conv, b_conv, w_out)


FFN_NAMES = ("ffn1", "ffn2")


def _trunk(x, p, w, tabs, ffn_w):
    bsz, seq, d = x.shape
    depth = p.shape[0]
    t = bsz * seq
    x = x.reshape(t, d)
    p = p.reshape(depth, t, p.shape[-1])
    tabs_even, tabs_odd = _sequence_dft_tables(seq, 0), _sequence_dft_tables(seq, 1)
    head_dim = tabs["head_dim"]

    def missing(keys):
        return [k for k in keys if k[2] < depth and k not in ffn_w]

    def sources(keys):
        return [(w[f"w_{name}_{part}"], layer) for name, part, layer in keys]

    def ffn(x, name, i):
        keys = missing([(name, "in", i + 1)]) if i % 2 == 0 else []
        x, cast = _ffn(x, w[f"g_{name}"], ffn_w[(name, "in", i)], ffn_w[(name, "out", i)],
                       i, sources(keys))
        ffn_w.update(zip(keys, cast))
        return x

    for i in range(depth):
        x = ffn(x, "ffn1", i)
        j = i // 2
        if i % 2 == 0:
            pe, po, u, vn = _mix_in(x, w["g_mix"], w["w_in_ab"], tabs["cs"], tabs["hm"],
                                    w["g_v"], i, j)
            ya = _seq_dft(pe.reshape(bsz, seq // 2, pe.shape[-1]),
                          po.reshape(bsz, seq // 2, po.shape[-1]), tabs_even, tabs_odd)
            x = _mix_out(x, ya.reshape(t, ya.shape[-1]), u, vn, w["w_s"], w["bias_s"],
                         w["w_out_ab"], j, head_dim)
        else:
            keys = missing([(n, part, i + 1) for n in FFN_NAMES for part in ("in", "out")])
            gb, z, cast = _conv_in(x, w["g_mix"], w["w_in_c"], i, j, sources(keys))
            ffn_w.update(zip(keys, cast))
            x = _conv_out(x, z, gb, w["w_conv"], w["b_conv"], w["w_out_c"], j, seq)
        x = ffn(x, "ffn2", i)
        x = _ple(x, p, w["g_ple"], w["w_ple_gate"], w["w_ple"], w["g_final"], i,
                 final=(i == depth - 1))
    return x.reshape(bsz, seq, d)


def kernel(x_prompt, x_sample, p_prompt, p_sample, g_ffn1, w_ffn1_in, w_ffn1_out, g_mix, w_in_ab, g_v, w_s, b_s, w_out_ab, w_in_c, w_conv, b_conv, w_out_c, g_ffn2, w_ffn2_in, w_ffn2_out, g_ple, w_ple_gate, w_ple, g_final):
    n_even, n_heads, head_dim = g_v.shape
    chunk = w_s.shape[-1]
    gain = lambda g: g[:, None, :]
    w = {
        "g_ffn1": gain(g_ffn1), "g_mix": gain(g_mix), "g_ffn2": gain(g_ffn2),
        "g_ple": gain(g_ple), "g_final": g_final[None, :],
        "g_v": g_v.reshape(n_even, 1, n_heads * head_dim),
        "w_ffn1_in": w_ffn1_in, "w_ffn1_out": w_ffn1_out,
        "w_ffn2_in": w_ffn2_in, "w_ffn2_out": w_ffn2_out,
        "w_in_ab": w_in_ab.astype(BF16), "w_out_ab": w_out_ab.astype(BF16),
        "w_s": w_s.astype(BF16).reshape(n_even, n_heads * chunk, chunk),
        "bias_s": jnp.repeat(jnp.swapaxes(b_s, 1, 2), head_dim, axis=2),
        "w_in_c": w_in_c.astype(BF16), "w_out_c": w_out_c.astype(BF16),
        "w_conv": w_conv, "b_conv": b_conv[:, None, :],
        "w_ple_gate": w_ple_gate.astype(BF16), "w_ple": w_ple.astype(BF16),
    }
    tabs = {"cs": _channel_dft_table(), "hm": _head_mean_table(head_dim), "head_dim": head_dim}
    depth = p_prompt.shape[0]
    ffn_w = {}
    for name in FFN_NAMES:
        for layer in range(depth):
            parts = ("in", "out") if layer == 0 else ("out",) if layer % 2 == 1 else ()
            for part in parts:
                ffn_w[(name, part, layer)] = w[f"w_{name}_{part}"][layer].astype(BF16)
    y_prompt = _trunk(x_prompt, p_prompt, w, tabs, ffn_w)
    y_sample = _trunk(x_sample, p_sample, w, tabs, ffn_w)
    return (y_prompt, y_sample)
```

```python
import functools
import math

import numpy as np
import jax
import jax.numpy as jnp
from jax import lax
from jax.experimental import pallas as pl
from jax.experimental.pallas import tpu as pltpu

EPS = 1e-6
A_HEAD_DIM = 64
BF16 = jnp.bfloat16
F32 = jnp.float32

V7X_VMEM_BYTES = 64 * 1024 * 1024
V7X_MXU_DIM = 256
V7X_LANES = 128
BF16_SUBLANES = 16
VMEM_RESERVE_BYTES = 6 * 1024 * 1024

TOKEN_TILE = 1024
MIX_IN_TOKEN_TILE = 512
FF_CHUNK = 512
CONV_IN_CHUNK = 512
DFT_ROW_TILE = 1024
DFT_K_CHUNK = 1024
CAST_BLOCK_BYTES = 4 * 1024 * 1024


def _vmem_limit(*nbytes):
    return int(min(sum(nbytes) + VMEM_RESERVE_BYTES, V7X_VMEM_BYTES - 2 * 1024 * 1024))


def _nbytes(shape, dtype):
    return int(np.prod(shape)) * jnp.dtype(dtype).itemsize


def _params(semantics, *nbytes):
    return pltpu.CompilerParams(dimension_semantics=semantics,
                                vmem_limit_bytes=_vmem_limit(*nbytes))


def _const_spec(block, index_map):
    return pl.BlockSpec(block, index_map, pipeline_mode=pl.Buffered(1))


def _rms(x, g):
    ms = jnp.mean(x * x, axis=-1, keepdims=True)
    return x * lax.rsqrt(ms + EPS) * g


def _dot(a, b):
    return jnp.dot(a, b, preferred_element_type=F32)


def _ffn_body(x_ref, g_ref, wg_ref, wu_ref, wo_ref, *rest, n_cast):
    cast_in, rest = rest[:n_cast], rest[n_cast:]
    o_ref, cast_out, h_ref = rest[0], rest[1:1 + n_cast], rest[1 + n_cast]
    j = pl.program_id(1)

    def half_swiglu(h):
        _cast_blocks(cast_in, cast_out)
        gate = _dot(h, wg_ref[...])
        up = _dot(h, wu_ref[...])
        act = (gate * jax.nn.sigmoid(gate) * up).astype(BF16)
        return 0.5 * _dot(act, wo_ref[...])

    @pl.when(j == 0)
    def _():
        x = x_ref[...]
        h = _rms(x, g_ref[...]).astype(BF16)
        h_ref[...] = h
        o_ref[...] = x + half_swiglu(h)

    @pl.when(j > 0)
    def _():
        o_ref[...] += half_swiglu(h_ref[...])


def _cast_rows(rows, steps):
    per = -(-rows // steps)
    per = -(-per // BF16_SUBLANES) * BF16_SUBLANES
    while per < rows and rows % per:
        per += BF16_SUBLANES
    return per if rows % per == 0 else None


class _CastJobs:
    def __init__(self, sources, steps, step_of):
        self.in_specs, self.out_specs, self.out_shape = [], [], []
        self.operands, self.nbytes = [], []
        for src, layer in sources:
            rows, cols = src.shape[1:]
            per = _cast_rows(rows, steps)
            assert per is not None
            last = rows // per - 1
            block = lambda *idx, last=last: jnp.minimum(step_of(*idx), last)
            self.in_specs.append(pl.BlockSpec(
                (None, per, cols), lambda *idx, block=block, layer=layer: (layer, block(*idx), 0)))
            self.out_specs.append(pl.BlockSpec(
                (per, cols), lambda *idx, block=block: (block(*idx), 0)))
            self.out_shape.append(jax.ShapeDtypeStruct((rows, cols), BF16))
            self.operands.append(src)
            self.nbytes += [2 * _nbytes((per, cols), F32), 2 * _nbytes((per, cols), BF16)]

    def __len__(self):
        return len(self.operands)


def _cast_blocks(cast_in, cast_out):
    for src, dst in zip(cast_in, cast_out):
        dst[...] = src[...].astype(BF16)


def _cast_body(src_ref, dst_ref):
    dst_ref[...] = src_ref[...].astype(BF16)


def _cast_layer(src, layer):
    rows, cols = src.shape[1:]
    per = _cast_rows(rows, max(1, _nbytes((rows, cols), F32) // CAST_BLOCK_BYTES))
    assert per is not None
    return pl.pallas_call(
        _cast_body,
        grid=(rows // per,),
        in_specs=[pl.BlockSpec((None, per, cols), lambda i: (layer, i, 0))],
        out_specs=pl.BlockSpec((per, cols), lambda i: (i, 0)),
        out_shape=jax.ShapeDtypeStruct((rows, cols), BF16),
        compiler_params=_params(
            ("parallel",), 2 * _nbytes((per, cols), F32), 2 * _nbytes((per, cols), BF16)),
        name="cast",
    )(src)


def _ffn(x, gains, w_in, w_out, layer, cast_sources=()):
    t, d = x.shape
    f = w_out.shape[0]
    tm, tk = TOKEN_TILE, FF_CHUNK
    nk = f // tk
    assert t % tm == 0 and f % tk == 0
    jobs = _CastJobs(cast_sources, (t // tm) * nk, lambda i, j: i * nk + j)
    outs = pl.pallas_call(
        functools.partial(_ffn_body, n_cast=len(jobs)),
        grid=(t // tm, nk),
        in_specs=[
            pl.BlockSpec((tm, d), lambda i, j: (i, 0)),
            pl.BlockSpec((None, 1, d), lambda i, j: (layer, 0, 0)),
            pl.BlockSpec((d, tk), lambda i, j: (0, j)),
            pl.BlockSpec((d, tk), lambda i, j: (0, j + nk)),
            pl.BlockSpec((tk, d), lambda i, j: (j, 0)),
        ] + jobs.in_specs,
        out_specs=[pl.BlockSpec((tm, d), lambda i, j: (i, 0))] + jobs.out_specs,
        out_shape=[jax.ShapeDtypeStruct((t, d), F32)] + jobs.out_shape,
        scratch_shapes=[pltpu.VMEM((tm, d), BF16)],
        compiler_params=_params(
            ("arbitrary", "arbitrary"),
            4 * _nbytes((tm, d), F32), _nbytes((tm, d), BF16),
            6 * _nbytes((d, tk), BF16), 3 * _nbytes((tm, tk), F32), *jobs.nbytes),
        name="ffn",
    )(x, gains, w_in, w_in, w_out, *jobs.operands)
    return outs[0], tuple(outs[1:])


def _ple_body(x_ref, p_ref, g_ref, wgate_ref, wp_ref, gf_ref, o_ref, *, final):
    x = x_ref[...]
    h = _rms(x, g_ref[...]).astype(BF16)
    gate = jax.nn.sigmoid(_dot(h, wgate_ref[...]))
    emb = _dot(p_ref[...].astype(BF16), wp_ref[...])
    y = x + gate * emb
    if final:
        y = _rms(y, gf_ref[...])
    o_ref[...] = y


def _ple(x, p, gains, w_gate, w_p, g_final, layer, final):
    t, d = x.shape
    e = p.shape[-1]
    tm = TOKEN_TILE
    return pl.pallas_call(
        functools.partial(_ple_body, final=final),
        grid=(t // tm,),
        in_specs=[
            pl.BlockSpec((tm, d), lambda i: (i, 0)),
            pl.BlockSpec((None, tm, e), lambda i: (layer, i, 0)),
            pl.BlockSpec((None, 1, d), lambda i: (layer, 0, 0)),
            _const_spec((None, d, d), lambda i: (layer, 0, 0)),
            _const_spec((None, e, d), lambda i: (layer, 0, 0)),
            pl.BlockSpec((1, d), lambda i: (0, 0)),
        ],
        out_specs=pl.BlockSpec((tm, d), lambda i: (i, 0)),
        out_shape=jax.ShapeDtypeStruct((t, d), F32),
        compiler_params=_params(
            ("parallel",),
            4 * _nbytes((tm, d), F32), 2 * _nbytes((tm, e), F32),
            _nbytes((d, d), BF16), _nbytes((e, d), BF16), 3 * _nbytes((tm, d), F32)),
        name="ple",
    )(x, p, gains, w_gate, w_p, g_final)


def _channel_dft_table():
    n = A_HEAD_DIM
    heads = V7X_MXU_DIM // n
    idx = np.arange(n)
    ang = 2.0 * np.pi * ((idx[:, None] * idx[None, :]) % n) / n
    eye = np.eye(heads)
    cos = np.kron(eye, np.cos(ang)) / math.sqrt(n)
    sin = np.kron(eye, np.sin(ang)) / math.sqrt(n)
    return jnp.asarray(np.concatenate([cos, sin], axis=1), dtype=BF16)


def _head_mean_table(head_dim):
    heads = V7X_MXU_DIM // head_dim
    m = np.kron(np.eye(heads), np.full((head_dim, head_dim), 1.0 / head_dim))
    return jnp.asarray(m, dtype=BF16)


def _sequence_dft_tables(s, parity):
    base = V7X_MXU_DIM
    half = s // 2
    pos = (2 * jnp.arange(half, dtype=jnp.int32) + parity)[None, :]
    lo = jnp.arange(base, dtype=jnp.int32)[:, None]
    hi = jnp.arange(half // base, dtype=jnp.int32)[:, None] * base
    step = F32(2.0 * math.pi / s)
    ang_lo = ((lo * pos) % s).astype(F32) * step
    ang_hi = ((hi * pos) % s).astype(F32) * step
    scale = F32(1.0 / math.sqrt(s))
    c_lo, s_lo = jnp.cos(ang_lo)[None], jnp.sin(ang_lo)[None]
    c_hi, s_hi = (jnp.cos(ang_hi) * scale)[:, None], (jnp.sin(ang_hi) * scale)[:, None]
    cos = (c_hi * c_lo - s_hi * s_lo).reshape(half, half)
    nsin = -(s_hi * c_lo + c_hi * s_lo).reshape(half, half)
    return cos.astype(BF16), nsin.astype(BF16)


def _mix_in_body(x_ref, g_ref, wa_ref, wu_ref, wv_ref, cs_ref, hm_ref, gv_ref,
                 pe_ref, po_ref, u_ref, vn_ref, p_ref):
    a = wa_ref.shape[1]
    w = V7X_MXU_DIM
    half = pe_ref.shape[0]
    h = _rms(x_ref[...], g_ref[...]).astype(BF16)
    za = _dot(h, wa_ref[...]).astype(BF16)
    lanes = p_ref.shape[-1]
    per_tile = w // lanes
    for t in range(a // w):
        pcs = _dot(za[:, t * w:(t + 1) * w], cs_ref[...])
        for c in range(2 * per_tile):
            n = (c // per_tile) * (a // lanes) + t * per_tile + c % per_tile
            p_ref[n] = pcs[:, c * lanes:(c + 1) * lanes]
    for n in range(p_ref.shape[0]):
        cols = slice(n * lanes, (n + 1) * lanes)
        pe_ref[:, cols] = p_ref[n, pl.ds(0, half, stride=2), :].astype(BF16)
        po_ref[:, cols] = p_ref[n, pl.ds(1, half, stride=2), :].astype(BF16)
    u_ref[...] = jax.nn.gelu(_dot(h, wu_ref[...]))
    v = jax.nn.gelu(_dot(h, wv_ref[...]))
    sq = (v * v).astype(BF16)
    for t in range(v.shape[1] // w):
        cols = slice(t * w, (t + 1) * w)
        ms = _dot(sq[:, cols], hm_ref[...])
        vn_ref[:, cols] = (v[:, cols] * lax.rsqrt(ms + EPS) * gv_ref[:, cols]).astype(BF16)


def _mix_in(x, gains, w_in, cs_tab, hm_tab, g_v, layer, j):
    t, d = x.shape
    a = d // 2
    b = d - a
    tm = MIX_IN_TOKEN_TILE
    assert w_in.shape[2] == a + 2 * b and a == b
    return pl.pallas_call(
        _mix_in_body,
        grid=(t // tm,),
        in_specs=[
            pl.BlockSpec((tm, d), lambda i: (i, 0)),
            pl.BlockSpec((None, 1, d), lambda i: (layer, 0, 0)),
            _const_spec((None, d, a), lambda i: (j, 0, 0)),
            _const_spec((None, d, b), lambda i: (j, 0, 1)),
            _const_spec((None, d, b), lambda i: (j, 0, 2)),
            _const_spec(cs_tab.shape, lambda i: (0, 0)),
            _const_spec(hm_tab.shape, lambda i: (0, 0)),
            pl.BlockSpec((None, 1, b), lambda i: (j, 0, 0)),
        ],
        out_specs=[
            pl.BlockSpec((tm // 2, 2 * a), lambda i: (i, 0)),
            pl.BlockSpec((tm // 2, 2 * a), lambda i: (i, 0)),
            pl.BlockSpec((tm, b), lambda i: (i, 0)),
            pl.BlockSpec((tm, b), lambda i: (i, 0)),
        ],
        out_shape=[
            jax.ShapeDtypeStruct((t // 2, 2 * a), BF16),
            jax.ShapeDtypeStruct((t // 2, 2 * a), BF16),
            jax.ShapeDtypeStruct((t, b), F32),
            jax.ShapeDtypeStruct((t, b), BF16),
        ],
        scratch_shapes=[pltpu.VMEM((2 * a // V7X_LANES, tm, V7X_LANES), F32)],
        compiler_params=_params(
            ("parallel",),
            2 * _nbytes((tm, d), F32), 3 * _nbytes((d, a), BF16),
            2 * _nbytes((tm, 2 * a), BF16), 2 * _nbytes((tm, b), F32),
            2 * _nbytes((tm, b), BF16), _nbytes((tm, 2 * a), F32),
            4 * _nbytes((tm, a), F32)),
        name="mix_in",
    )(x, gains, w_in, w_in, w_in, cs_tab, hm_tab, g_v)


def _seq_dft_body(ce_ref, se_ref, co_ref, so_ref, pe_ref, po_ref, o_ref, *acc, k_steps):
    a = o_ref.shape[-1]
    kk = pl.program_id(2)
    if k_steps > 1:
        acc_e, acc_o = acc

        @pl.when(kk == 0)
        def _():
            acc_e[...] = jnp.zeros_like(acc_e)
            acc_o[...] = jnp.zeros_like(acc_o)

    even = _dot(ce_ref[...], pe_ref[:, :a]) + _dot(se_ref[...], pe_ref[:, a:])
    odd = _dot(co_ref[...], po_ref[:, :a]) + _dot(so_ref[...], po_ref[:, a:])
    if k_steps == 1:
        o_ref[0] = (even + odd).astype(BF16)
        o_ref[1] = (even - odd).astype(BF16)
        return
    acc_e[...] += even
    acc_o[...] += odd

    @pl.when(kk == k_steps - 1)
    def _():
        o_ref[0] = (acc_e[...] + acc_o[...]).astype(BF16)
        o_ref[1] = (acc_e[...] - acc_o[...]).astype(BF16)


def _seq_dft(pe, po, tabs_even, tabs_odd):
    bsz, half, a2 = pe.shape
    a = a2 // 2
    tr = min(DFT_ROW_TILE, half)
    tc = min(DFT_K_CHUNK, half)
    k_steps = half // tc
    tab_spec = pl.BlockSpec((tr, tc), lambda b, r, k: (r, k))
    p_spec = pl.BlockSpec((None, tc, a2), lambda b, r, k: (b, k, 0))
    return pl.pallas_call(
        functools.partial(_seq_dft_body, k_steps=k_steps),
        grid=(bsz, half // tr, k_steps),
        in_specs=[tab_spec, tab_spec, tab_spec, tab_spec, p_spec, p_spec],
        out_specs=pl.BlockSpec((None, 2, tr, a), lambda b, r, k: (b, 0, r, 0)),
        out_shape=jax.ShapeDtypeStruct((bsz, 2, half, a), BF16),
        scratch_shapes=[pltpu.VMEM((tr, a), F32)] * 2 if k_steps > 1 else [],
        compiler_params=_params(
            ("parallel", "parallel", "arbitrary"),
            8 * _nbytes((tr, tc), BF16), 4 * _nbytes((tc, a2), BF16),
            2 * _nbytes((2, tr, a), BF16), 4 * _nbytes((tr, a), F32)),
        name="seq_dft",
    )(*tabs_even, *tabs_odd, pe, po)


def _mix_out_body(x_ref, ya_ref, u_ref, vn_ref, ws_ref, bias_ref, wo_ref, o_ref, y_ref,
                  *, head_dim):
    tm, a = ya_ref.shape
    chunk = ws_ref.shape[1]
    w = V7X_MXU_DIM
    heads = w // head_dim
    lane_head = lax.broadcasted_iota(jnp.int32, (chunk, w), 1) // head_dim
    y_ref[:, :a] = ya_ref[...]
    for c in range(tm // chunk):
        rows = slice(c * chunk, (c + 1) * chunk)
        for q in range(vn_ref.shape[1] // w):
            cols = slice(q * w, (q + 1) * w)
            stacked = _dot(ws_ref[q * heads * chunk:(q + 1) * heads * chunk, :],
                           vn_ref[rows, cols])
            mixed = stacked[:chunk]
            for r in range(1, heads):
                mixed = jnp.where(lane_head == r, stacked[r * chunk:(r + 1) * chunk], mixed)
            yb = u_ref[rows, cols] * (mixed + bias_ref[:, cols])
            y_ref[rows, a + q * w:a + (q + 1) * w] = yb.astype(BF16)
    o_ref[...] = x_ref[...] + _dot(y_ref[...], wo_ref[...])


def _mix_out(x, ya, u, vn, w_s, bias, w_out, j, head_dim):
    t, d = x.shape
    a = ya.shape[1]
    b = u.shape[1]
    chunk = w_s.shape[2]
    tm = TOKEN_TILE
    assert tm % chunk == 0 and b % V7X_MXU_DIM == 0 and V7X_MXU_DIM % head_dim == 0
    return pl.pallas_call(
        functools.partial(_mix_out_body, head_dim=head_dim),
        grid=(t // tm,),
        in_specs=[
            pl.BlockSpec((tm, d), lambda i: (i, 0)),
            pl.BlockSpec((tm, a), lambda i: (i, 0)),
            pl.BlockSpec((tm, b), lambda i: (i, 0)),
            pl.BlockSpec((tm, b), lambda i: (i, 0)),
            _const_spec((None,) + w_s.shape[1:], lambda i: (j, 0, 0)),
            _const_spec((None,) + bias.shape[1:], lambda i: (j, 0, 0)),
            _const_spec((None, a + b, d), lambda i: (j, 0, 0)),
        ],
        out_specs=pl.BlockSpec((tm, d), lambda i: (i, 0)),
        out_shape=jax.ShapeDtypeStruct((t, d), F32),
        scratch_shapes=[pltpu.VMEM((tm, a + b), BF16)],
        compiler_params=_params(
            ("parallel",),
            4 * _nbytes((tm, d), F32), 2 * _nbytes((tm, a), BF16),
            2 * _nbytes((tm, b), F32), 2 * _nbytes((tm, b), BF16),
            _nbytes(w_s.shape[1:], BF16), _nbytes(bias.shape[1:], F32),
            _nbytes((a + b, d), BF16), _nbytes((tm, a + b), BF16),
            2 * _nbytes((tm, d), F32)),
        name="mix_out",
    )(x, ya, u, vn, w_s, bias, w_out)


def _conv_in_body(x_ref, g_ref, wb_ref, wc_ref, wx_ref, *rest, n_cast):
    cast_in, rest = rest[:n_cast], rest[n_cast:]
    gb_ref, z_ref, cast_out, h_ref = rest[0], rest[1], rest[2:2 + n_cast], rest[2 + n_cast]

    def project(h):
        _cast_blocks(cast_in, cast_out)
        gb_ref[...] = _dot(h, wb_ref[...]).astype(BF16)
        z_ref[...] = (_dot(h, wc_ref[...]) * _dot(h, wx_ref[...])).astype(BF16)

    @pl.when(pl.program_id(1) == 0)
    def _():
        h = _rms(x_ref[...], g_ref[...]).astype(BF16)
        h_ref[...] = h
        project(h)

    @pl.when(pl.program_id(1) > 0)
    def _():
        project(h_ref[...])


def _conv_in(x, gains, w_in, layer, j, cast_sources=()):
    t, d = x.shape
    c = w_in.shape[2] // 3
    tm, tn = TOKEN_TILE, CONV_IN_CHUNK
    nn = c // tn
    jobs = _CastJobs(cast_sources, (t // tm) * nn, lambda i, n: i * nn + n)
    outs = pl.pallas_call(
        functools.partial(_conv_in_body, n_cast=len(jobs)),
        grid=(t // tm, nn),
        in_specs=[
            pl.BlockSpec((tm, d), lambda i, n: (i, 0)),
            pl.BlockSpec((None, 1, d), lambda i, n: (layer, 0, 0)),
            pl.BlockSpec((None, d, tn), lambda i, n: (j, 0, n)),
            pl.BlockSpec((None, d, tn), lambda i, n: (j, 0, n + nn)),
            pl.BlockSpec((None, d, tn), lambda i, n: (j, 0, n + 2 * nn)),
        ] + jobs.in_specs,
        out_specs=[
            pl.BlockSpec((tm, tn), lambda i, n: (i, n)),
            pl.BlockSpec((tm, tn), lambda i, n: (i, n)),
        ] + jobs.out_specs,
        out_shape=[jax.ShapeDtypeStruct((t, c), BF16),
                   jax.ShapeDtypeStruct((t, c), BF16)] + jobs.out_shape,
        scratch_shapes=[pltpu.VMEM((tm, d), BF16)],
        compiler_params=_params(
            ("arbitrary", "arbitrary"),
            2 * _nbytes((tm, d), F32), _nbytes((tm, d), BF16),
            6 * _nbytes((d, tn), BF16), 4 * _nbytes((tm, tn), BF16),
            3 * _nbytes((tm, tn), F32), *jobs.nbytes),
        name="conv_in",
    )(x, gains, w_in, w_in, w_in, *jobs.operands)
    return outs[0], outs[1], tuple(outs[2:])


def _conv_out_body(x_ref, z_ref, zp_ref, zn_ref, gb_ref, wc_ref, bc_ref, wo_ref, o_ref,
                   *, tiles_per_seq):
    tm = z_ref.shape[0]
    i = pl.program_id(0) % tiles_per_seq
    z = z_ref[...].astype(F32)
    halo = zp_ref.shape[0]
    before = jnp.where(i == 0, 0.0, zp_ref[...].astype(F32)[halo - 1:halo])
    after = jnp.where(i == tiles_per_seq - 1, 0.0, zn_ref[...].astype(F32)[0:1])
    row = lax.broadcasted_iota(jnp.int32, z.shape, 0)
    z_prev = jnp.where(row == 0, before, pltpu.roll(z, 1, axis=0))
    z_next = jnp.where(row == tm - 1, after, pltpu.roll(z, tm - 1, axis=0))
    conv = z_prev * wc_ref[0:1] + z * wc_ref[1:2] + z_next * wc_ref[2:3] + bc_ref[...]
    y = (gb_ref[...].astype(F32) * conv).astype(BF16)
    o_ref[...] = x_ref[...] + _dot(y, wo_ref[...])


def _conv_out(x, z, gb, w_conv, b_conv, w_out, j, seq):
    t, d = x.shape
    c = z.shape[1]
    tm = TOKEN_TILE
    halo = 16
    assert seq % tm == 0 and tm % halo == 0
    per = tm // halo
    last = t // halo - 1
    return pl.pallas_call(
        functools.partial(_conv_out_body, tiles_per_seq=seq // tm),
        grid=(t // tm,),
        in_specs=[
            pl.BlockSpec((tm, d), lambda i: (i, 0)),
            pl.BlockSpec((tm, c), lambda i: (i, 0)),
            pl.BlockSpec((halo, c), lambda i: (jnp.maximum(i * per - 1, 0), 0)),
            pl.BlockSpec((halo, c), lambda i: (jnp.minimum((i + 1) * per, last), 0)),
            pl.BlockSpec((tm, c), lambda i: (i, 0)),
            pl.BlockSpec((None,) + w_conv.shape[1:], lambda i: (j, 0, 0)),
            pl.BlockSpec((None, 1, c), lambda i: (j, 0, 0)),
            _const_spec((None, c, d), lambda i: (j, 0, 0)),
        ],
        out_specs=pl.BlockSpec((tm, d), lambda i: (i, 0)),
        out_shape=jax.ShapeDtypeStruct((t, d), F32),
        compiler_params=_params(
            ("parallel",),
            4 * _nbytes((tm, d), F32), 4 * _nbytes((tm, c), BF16),
            _nbytes((c, d), BF16), 5 * _nbytes((tm, c), F32)),
        name="conv_out",
    )(x, z, z, z, gb, w_conv, b_conv, w_out)


FFN_NAMES = ("ffn1", "ffn2")


def _trunk(x, p, w, tabs, ffn_w):
    bsz, seq, d = x.shape
    depth = p.shape[0]
    t = bsz * seq
    x = x.reshape(t, d)
    p = p.reshape(depth, t, p.shape[-1])
    tabs_even, tabs_odd = _sequence_dft_tables(seq, 0), _sequence_dft_tables(seq, 1)
    head_dim = tabs["head_dim"]

    def missing(keys):
        return [k for k in keys if k[2] < depth and k not in ffn_w]

    def sources(keys):
        return [(w[f"w_{name}_{part}"], layer) for name, part, layer in keys]

    def ffn(x, name, i):
        keys = missing([(name, part, i + 1) for part in ("in", "out")]) if i % 2 == 0 else []
        x, cast = _ffn(x, w[f"g_{name}"], ffn_w[(name, "in", i)], ffn_w[(name, "out", i)],
                       i, sources(keys))
        ffn_w.update(zip(keys, cast))
        return x

    for i in range(depth):
        x = ffn(x, "ffn1", i)
        j = i // 2
        if i % 2 == 0:
            pe, po, u, vn = _mix_in(x, w["g_mix"], w["w_in_ab"], tabs["cs"], tabs["hm"],
                                    w["g_v"], i, j)
            ya = _seq_dft(pe.reshape(bsz, seq // 2, pe.shape[-1]),
                          po.reshape(bsz, seq // 2, po.shape[-1]), tabs_even, tabs_odd)
            x = _mix_out(x, ya.reshape(t, ya.shape[-1]), u, vn, w["w_s"], w["bias_s"],
                         w["w_out_ab"], j, head_dim)
        else:
            keys = missing([(n, part, i + 1) for n in FFN_NAMES for part in ("in", "out")])
            gb, z, cast = _conv_in(x, w["g_mix"], w["w_in_c"], i, j, sources(keys))
            ffn_w.update(zip(keys, cast))
            x = _conv_out(x, z, gb, w["w_conv"], w["b_conv"], w["w_out_c"], j, seq)
        x = ffn(x, "ffn2", i)
        x = _ple(x, p, w["g_ple"], w["w_ple_gate"], w["w_ple"], w["g_final"], i,
                 final=(i == depth - 1))
    return x.reshape(bsz, seq, d)


def kernel(x_prompt, x_sample, p_prompt, p_sample, g_ffn1, w_ffn1_in, w_ffn1_out, g_mix, w_in_ab, g_v, w_s, b_s, w_out_ab, w_in_c, w_conv, b_conv, w_out_c, g_ffn2, w_ffn2_in, w_ffn2_out, g_ple, w_ple_gate, w_ple, g_final):
    n_even, n_heads, head_dim = g_v.shape
    chunk = w_s.shape[-1]
    gain = lambda g: g[:, None, :]
    w = {
        "g_ffn1": gain(g_ffn1), "g_mix": gain(g_mix), "g_ffn2": gain(g_ffn2),
        "g_ple": gain(g_ple), "g_final": g_final[None, :],
        "g_v": g_v.reshape(n_even, 1, n_heads * head_dim),
        "w_ffn1_in": w_ffn1_in, "w_ffn1_out": w_ffn1_out,
        "w_ffn2_in": w_ffn2_in, "w_ffn2_out": w_ffn2_out,
        "w_in_ab": w_in_ab.astype(BF16), "w_out_ab": w_out_ab.astype(BF16),
        "w_s": w_s.astype(BF16).reshape(n_even, n_heads * chunk, chunk),
        "bias_s": jnp.repeat(jnp.swapaxes(b_s, 1, 2), head_dim, axis=2),
        "w_in_c": w_in_c.astype(BF16), "w_out_c": w_out_c.astype(BF16),
        "w_conv": w_conv, "b_conv": b_conv[:, None, :],
        "w_ple_gate": w_ple_gate.astype(BF16), "w_ple": w_ple.astype(BF16),
    }
    tabs = {"cs": _channel_dft_table(), "hm": _head_mean_table(head_dim), "head_dim": head_dim}
    ffn_w = {(name, part, 0): _cast_layer(w[f"w_{name}_{part}"], 0)
             for name in FFN_NAMES for part in ("in", "out")}
    y_prompt = _trunk(x_prompt, p_prompt, w, tabs, ffn_w)
    y_sample = _trunk(x_sample, p_sample, w, tabs, ffn_w)
    return (y_prompt, y_sample)
```

```python
import functools
import math

import numpy as np
import jax
import jax.numpy as jnp
from jax import lax
from jax.experimental import pallas as pl
from jax.experimental.pallas import tpu as pltpu

EPS = 1e-6
A_HEAD_DIM = 64
BF16 = jnp.bfloat16
F32 = jnp.float32

V7X_VMEM_BYTES = 64 * 1024 * 1024
V7X_MXU_DIM = 256
V7X_LANES = 128
BF16_SUBLANES = 16
VMEM_RESERVE_BYTES = 6 * 1024 * 1024

TOKEN_TILE = 1024
MIX_IN_TOKEN_TILE = 512
FF_CHUNK = 512
CONV_IN_CHUNK = 512
DFT_ROW_TILE = 1024
DFT_K_CHUNK = 1024
CAST_BLOCK_BYTES = 4 * 1024 * 1024


def _vmem_limit(*nbytes):
    return int(min(sum(nbytes) + VMEM_RESERVE_BYTES, V7X_VMEM_BYTES - 2 * 1024 * 1024))


def _nbytes(shape, dtype):
    return int(np.prod(shape)) * jnp.dtype(dtype).itemsize


def _params(semantics, *nbytes):
    return pltpu.CompilerParams(dimension_semantics=semantics,
                                vmem_limit_bytes=_vmem_limit(*nbytes))


def _const_spec(block, index_map):
    return pl.BlockSpec(block, index_map, pipeline_mode=pl.Buffered(1))


def _rms(x, g):
    ms = jnp.mean(x * x, axis=-1, keepdims=True)
    return x * lax.rsqrt(ms + EPS) * g


def _dot(a, b):
    return jnp.dot(a, b, preferred_element_type=F32)


def _ffn_body(x_ref, g_ref, wg_ref, wu_ref, wo_ref, o_ref, h_ref):
    j = pl.program_id(1)

    def half_swiglu(h):
        gate = _dot(h, wg_ref[...])
        up = _dot(h, wu_ref[...])
        act = (gate * jax.nn.sigmoid(gate) * up).astype(BF16)
        return 0.5 * _dot(act, wo_ref[...])

    @pl.when(j == 0)
    def _():
        x = x_ref[...]
        h = _rms(x, g_ref[...]).astype(BF16)
        h_ref[...] = h
        o_ref[...] = x + half_swiglu(h)

    @pl.when(j > 0)
    def _():
        o_ref[...] += half_swiglu(h_ref[...])


def _cast_rows(rows, steps):
    per = -(-rows // steps)
    per = -(-per // BF16_SUBLANES) * BF16_SUBLANES
    while per < rows and rows % per:
        per += BF16_SUBLANES
    return per if rows % per == 0 else None


class _CastJobs:
    def __init__(self, sources, steps, step_of):
        self.in_specs, self.out_specs, self.out_shape = [], [], []
        self.operands, self.nbytes = [], []
        for src, layer in sources:
            rows, cols = src.shape[1:]
            per = _cast_rows(rows, steps)
            assert per is not None
            last = rows // per - 1
            block = lambda *idx, last=last: jnp.minimum(step_of(*idx), last)
            self.in_specs.append(pl.BlockSpec(
                (None, per, cols), lambda *idx, block=block, layer=layer: (layer, block(*idx), 0)))
            self.out_specs.append(pl.BlockSpec(
                (per, cols), lambda *idx, block=block: (block(*idx), 0)))
            self.out_shape.append(jax.ShapeDtypeStruct((rows, cols), BF16))
            self.operands.append(src)
            self.nbytes += [2 * _nbytes((per, cols), F32), 2 * _nbytes((per, cols), BF16)]

    def __len__(self):
        return len(self.operands)


def _cast_blocks(cast_in, cast_out):
    for src, dst in zip(cast_in, cast_out):
        dst[...] = src[...].astype(BF16)


def _cast_body(src_ref, dst_ref):
    dst_ref[...] = src_ref[...].astype(BF16)


def _cast_layer(src, layer):
    rows, cols = src.shape[1:]
    per = _cast_rows(rows, max(1, _nbytes((rows, cols), F32) // CAST_BLOCK_BYTES))
    assert per is not None
    return pl.pallas_call(
        _cast_body,
        grid=(rows // per,),
        in_specs=[pl.BlockSpec((None, per, cols), lambda i: (layer, i, 0))],
        out_specs=pl.BlockSpec((per, cols), lambda i: (i, 0)),
        out_shape=jax.ShapeDtypeStruct((rows, cols), BF16),
        compiler_params=_params(
            ("parallel",), 2 * _nbytes((per, cols), F32), 2 * _nbytes((per, cols), BF16)),
        name="cast",
    )(src)


def _ffn(x, gains, w_in, w_out, layer):
    t, d = x.shape
    f = w_out.shape[0]
    tm, tk = TOKEN_TILE, FF_CHUNK
    nk = f // tk
    assert t % tm == 0 and f % tk == 0
    return pl.pallas_call(
        _ffn_body,
        grid=(t // tm, nk),
        in_specs=[
            pl.BlockSpec((tm, d), lambda i, j: (i, 0)),
            pl.BlockSpec((None, 1, d), lambda i, j: (layer, 0, 0)),
            pl.BlockSpec((d, tk), lambda i, j: (0, j)),
            pl.BlockSpec((d, tk), lambda i, j: (0, j + nk)),
            pl.BlockSpec((tk, d), lambda i, j: (j, 0)),
        ],
        out_specs=pl.BlockSpec((tm, d), lambda i, j: (i, 0)),
        out_shape=jax.ShapeDtypeStruct((t, d), F32),
        scratch_shapes=[pltpu.VMEM((tm, d), BF16)],
        compiler_params=_params(
            ("parallel", "arbitrary"),
            4 * _nbytes((tm, d), F32), _nbytes((tm, d), BF16),
            6 * _nbytes((d, tk), BF16), 3 * _nbytes((tm, tk), F32)),
        name="ffn",
    )(x, gains, w_in, w_in, w_out)


def _ple_body(x_ref, p_ref, g_ref, wgate_ref, wp_ref, gf_ref, o_ref, *, final):
    x = x_ref[...]
    h = _rms(x, g_ref[...]).astype(BF16)
    gate = jax.nn.sigmoid(_dot(h, wgate_ref[...]))
    emb = _dot(p_ref[...].astype(BF16), wp_ref[...])
    y = x + gate * emb
    if final:
        y = _rms(y, gf_ref[...])
    o_ref[...] = y


def _ple(x, p, gains, w_gate, w_p, g_final, layer, final):
    t, d = x.shape
    e = p.shape[-1]
    tm = TOKEN_TILE
    return pl.pallas_call(
        functools.partial(_ple_body, final=final),
        grid=(t // tm,),
        in_specs=[
            pl.BlockSpec((tm, d), lambda i: (i, 0)),
            pl.BlockSpec((None, tm, e), lambda i: (layer, i, 0)),
            pl.BlockSpec((None, 1, d), lambda i: (layer, 0, 0)),
            _const_spec((None, d, d), lambda i: (layer, 0, 0)),
            _const_spec((None, e, d), lambda i: (layer, 0, 0)),
            pl.BlockSpec((1, d), lambda i: (0, 0)),
        ],
        out_specs=pl.BlockSpec((tm, d), lambda i: (i, 0)),
        out_shape=jax.ShapeDtypeStruct((t, d), F32),
        compiler_params=_params(
            ("parallel",),
            4 * _nbytes((tm, d), F32), 2 * _nbytes((tm, e), F32),
            _nbytes((d, d), BF16), _nbytes((e, d), BF16), 3 * _nbytes((tm, d), F32)),
        name="ple",
    )(x, p, gains, w_gate, w_p, g_final)


def _channel_dft_table():
    n = A_HEAD_DIM
    heads = V7X_MXU_DIM // n
    idx = np.arange(n)
    ang = 2.0 * np.pi * ((idx[:, None] * idx[None, :]) % n) / n
    eye = np.eye(heads)
    cos = np.kron(eye, np.cos(ang)) / math.sqrt(n)
    sin = np.kron(eye, np.sin(ang)) / math.sqrt(n)
    return jnp.asarray(np.concatenate([cos, sin], axis=1), dtype=BF16)


def _head_mean_table(head_dim):
    heads = V7X_MXU_DIM // head_dim
    m = np.kron(np.eye(heads), np.full((head_dim, head_dim), 1.0 / head_dim))
    return jnp.asarray(m, dtype=BF16)


def _sequence_dft_tables(s, parity):
    base = V7X_MXU_DIM
    half = s // 2
    pos = (2 * jnp.arange(half, dtype=jnp.int32) + parity)[None, :]
    lo = jnp.arange(base, dtype=jnp.int32)[:, None]
    hi = jnp.arange(half // base, dtype=jnp.int32)[:, None] * base
    step = F32(2.0 * math.pi / s)
    ang_lo = ((lo * pos) % s).astype(F32) * step
    ang_hi = ((hi * pos) % s).astype(F32) * step
    scale = F32(1.0 / math.sqrt(s))
    c_lo, s_lo = jnp.cos(ang_lo)[None], jnp.sin(ang_lo)[None]
    c_hi, s_hi = (jnp.cos(ang_hi) * scale)[:, None], (jnp.sin(ang_hi) * scale)[:, None]
    cos = (c_hi * c_lo - s_hi * s_lo).reshape(half, half)
    nsin = -(s_hi * c_lo + c_hi * s_lo).reshape(half, half)
    return cos.astype(BF16), nsin.astype(BF16)


def _mix_in_body(x_ref, g_ref, wa_ref, wu_ref, wv_ref, cs_ref, hm_ref, gv_ref, *rest, n_cast):
    cast_in, rest = rest[:n_cast], rest[n_cast:]
    pe_ref, po_ref, u_ref, vn_ref = rest[:4]
    cast_out, p_ref = rest[4:4 + n_cast], rest[4 + n_cast]
    _cast_blocks(cast_in, cast_out)
    a = wa_ref.shape[1]
    w = V7X_MXU_DIM
    half = pe_ref.shape[0]
    h = _rms(x_ref[...], g_ref[...]).astype(BF16)
    za = _dot(h, wa_ref[...]).astype(BF16)
    lanes = p_ref.shape[-1]
    per_tile = w // lanes
    for t in range(a // w):
        pcs = _dot(za[:, t * w:(t + 1) * w], cs_ref[...])
        for c in range(2 * per_tile):
            n = (c // per_tile) * (a // lanes) + t * per_tile + c % per_tile
            p_ref[n] = pcs[:, c * lanes:(c + 1) * lanes]
    for n in range(p_ref.shape[0]):
        cols = slice(n * lanes, (n + 1) * lanes)
        pe_ref[:, cols] = p_ref[n, pl.ds(0, half, stride=2), :].astype(BF16)
        po_ref[:, cols] = p_ref[n, pl.ds(1, half, stride=2), :].astype(BF16)
    u_ref[...] = jax.nn.gelu(_dot(h, wu_ref[...]))
    v = jax.nn.gelu(_dot(h, wv_ref[...]))
    sq = (v * v).astype(BF16)
    for t in range(v.shape[1] // w):
        cols = slice(t * w, (t + 1) * w)
        ms = _dot(sq[:, cols], hm_ref[...])
        vn_ref[:, cols] = (v[:, cols] * lax.rsqrt(ms + EPS) * gv_ref[:, cols]).astype(BF16)


def _mix_in(x, gains, w_in, cs_tab, hm_tab, g_v, layer, j, cast_sources=()):
    t, d = x.shape
    a = d // 2
    b = d - a
    tm = MIX_IN_TOKEN_TILE
    assert w_in.shape[2] == a + 2 * b and a == b
    jobs = _CastJobs(cast_sources, t // tm, lambda i: i)
    outs = pl.pallas_call(
        functools.partial(_mix_in_body, n_cast=len(jobs)),
        grid=(t // tm,),
        in_specs=[
            pl.BlockSpec((tm, d), lambda i: (i, 0)),
            pl.BlockSpec((None, 1, d), lambda i: (layer, 0, 0)),
            _const_spec((None, d, a), lambda i: (j, 0, 0)),
            _const_spec((None, d, b), lambda i: (j, 0, 1)),
            _const_spec((None, d, b), lambda i: (j, 0, 2)),
            _const_spec(cs_tab.shape, lambda i: (0, 0)),
            _const_spec(hm_tab.shape, lambda i: (0, 0)),
            pl.BlockSpec((None, 1, b), lambda i: (j, 0, 0)),
        ] + jobs.in_specs,
        out_specs=[
            pl.BlockSpec((tm // 2, 2 * a), lambda i: (i, 0)),
            pl.BlockSpec((tm // 2, 2 * a), lambda i: (i, 0)),
            pl.BlockSpec((tm, b), lambda i: (i, 0)),
            pl.BlockSpec((tm, b), lambda i: (i, 0)),
        ] + jobs.out_specs,
        out_shape=[
            jax.ShapeDtypeStruct((t // 2, 2 * a), BF16),
            jax.ShapeDtypeStruct((t // 2, 2 * a), BF16),
            jax.ShapeDtypeStruct((t, b), F32),
            jax.ShapeDtypeStruct((t, b), BF16),
        ] + jobs.out_shape,
        scratch_shapes=[pltpu.VMEM((2 * a // V7X_LANES, tm, V7X_LANES), F32)],
        compiler_params=_params(
            ("arbitrary",),
            2 * _nbytes((tm, d), F32), 3 * _nbytes((d, a), BF16),
            2 * _nbytes((tm, 2 * a), BF16), 2 * _nbytes((tm, b), F32),
            2 * _nbytes((tm, b), BF16), _nbytes((tm, 2 * a), F32),
            4 * _nbytes((tm, a), F32), *jobs.nbytes),
        name="mix_in",
    )(x, gains, w_in, w_in, w_in, cs_tab, hm_tab, g_v, *jobs.operands)
    return outs[0], outs[1], outs[2], outs[3], tuple(outs[4:])


def _seq_dft_body(ce_ref, se_ref, co_ref, so_ref, pe_ref, po_ref, o_ref, *acc, k_steps):
    a = o_ref.shape[-1]
    kk = pl.program_id(2)
    if k_steps > 1:
        acc_e, acc_o = acc

        @pl.when(kk == 0)
        def _():
            acc_e[...] = jnp.zeros_like(acc_e)
            acc_o[...] = jnp.zeros_like(acc_o)

    even = _dot(ce_ref[...], pe_ref[:, :a]) + _dot(se_ref[...], pe_ref[:, a:])
    odd = _dot(co_ref[...], po_ref[:, :a]) + _dot(so_ref[...], po_ref[:, a:])
    if k_steps == 1:
        o_ref[0] = (even + odd).astype(BF16)
        o_ref[1] = (even - odd).astype(BF16)
        return
    acc_e[...] += even
    acc_o[...] += odd

    @pl.when(kk == k_steps - 1)
    def _():
        o_ref[0] = (acc_e[...] + acc_o[...]).astype(BF16)
        o_ref[1] = (acc_e[...] - acc_o[...]).astype(BF16)


def _seq_dft(pe, po, tabs_even, tabs_odd):
    bsz, half, a2 = pe.shape
    a = a2 // 2
    tr = min(DFT_ROW_TILE, half)
    tc = min(DFT_K_CHUNK, half)
    k_steps = half // tc
    tab_spec = pl.BlockSpec((tr, tc), lambda b, r, k: (r, k))
    p_spec = pl.BlockSpec((None, tc, a2), lambda b, r, k: (b, k, 0))
    return pl.pallas_call(
        functools.partial(_seq_dft_body, k_steps=k_steps),
        grid=(bsz, half // tr, k_steps),
        in_specs=[tab_spec, tab_spec, tab_spec, tab_spec, p_spec, p_spec],
        out_specs=pl.BlockSpec((None, 2, tr, a), lambda b, r, k: (b, 0, r, 0)),
        out_shape=jax.ShapeDtypeStruct((bsz, 2, half, a), BF16),
        scratch_shapes=[pltpu.VMEM((tr, a), F32)] * 2 if k_steps > 1 else [],
        compiler_params=_params(
            ("parallel", "parallel", "arbitrary"),
            8 * _nbytes((tr, tc), BF16), 4 * _nbytes((tc, a2), BF16),
            2 * _nbytes((2, tr, a), BF16), 4 * _nbytes((tr, a), F32)),
        name="seq_dft",
    )(*tabs_even, *tabs_odd, pe, po)


def _mix_out_body(x_ref, ya_ref, u_ref, vn_ref, ws_ref, bias_ref, wo_ref, o_ref, y_ref,
                  *, head_dim):
    tm, a = ya_ref.shape
    chunk = ws_ref.shape[1]
    w = V7X_MXU_DIM
    heads = w // head_dim
    lane_head = lax.broadcasted_iota(jnp.int32, (chunk, w), 1) // head_dim
    y_ref[:, :a] = ya_ref[...]
    for c in range(tm // chunk):
        rows = slice(c * chunk, (c + 1) * chunk)
        for q in range(vn_ref.shape[1] // w):
            cols = slice(q * w, (q + 1) * w)
            stacked = _dot(ws_ref[q * heads * chunk:(q + 1) * heads * chunk, :],
                           vn_ref[rows, cols])
            mixed = stacked[:chunk]
            for r in range(1, heads):
                mixed = jnp.where(lane_head == r, stacked[r * chunk:(r + 1) * chunk], mixed)
            yb = u_ref[rows, cols] * (mixed + bias_ref[:, cols])
            y_ref[rows, a + q * w:a + (q + 1) * w] = yb.astype(BF16)
    o_ref[...] = x_ref[...] + _dot(y_ref[...], wo_ref[...])


def _mix_out(x, ya, u, vn, w_s, bias, w_out, j, head_dim):
    t, d = x.shape
    a = ya.shape[1]
    b = u.shape[1]
    chunk = w_s.shape[2]
    tm = TOKEN_TILE
    assert tm % chunk == 0 and b % V7X_MXU_DIM == 0 and V7X_MXU_DIM % head_dim == 0
    return pl.pallas_call(
        functools.partial(_mix_out_body, head_dim=head_dim),
        grid=(t // tm,),
        in_specs=[
            pl.BlockSpec((tm, d), lambda i: (i, 0)),
            pl.BlockSpec((tm, a), lambda i: (i, 0)),
            pl.BlockSpec((tm, b), lambda i: (i, 0)),
            pl.BlockSpec((tm, b), lambda i: (i, 0)),
            _const_spec((None,) + w_s.shape[1:], lambda i: (j, 0, 0)),
            _const_spec((None,) + bias.shape[1:], lambda i: (j, 0, 0)),
            _const_spec((None, a + b, d), lambda i: (j, 0, 0)),
        ],
        out_specs=pl.BlockSpec((tm, d), lambda i: (i, 0)),
        out_shape=jax.ShapeDtypeStruct((t, d), F32),
        scratch_shapes=[pltpu.VMEM((tm, a + b), BF16)],
        compiler_params=_params(
            ("parallel",),
            4 * _nbytes((tm, d), F32), 2 * _nbytes((tm, a), BF16),
            2 * _nbytes((tm, b), F32), 2 * _nbytes((tm, b), BF16),
            _nbytes(w_s.shape[1:], BF16), _nbytes(bias.shape[1:], F32),
            _nbytes((a + b, d), BF16), _nbytes((tm, a + b), BF16),
            2 * _nbytes((tm, d), F32)),
        name="mix_out",
    )(x, ya, u, vn, w_s, bias, w_out)


def _conv_in_body(x_ref, g_ref, wb_ref, wc_ref, wx_ref, *rest, n_cast):
    cast_in, rest = rest[:n_cast], rest[n_cast:]
    gb_ref, z_ref, cast_out, h_ref = rest[0], rest[1], rest[2:2 + n_cast], rest[2 + n_cast]

    def project(h):
        _cast_blocks(cast_in, cast_out)
        gb_ref[...] = _dot(h, wb_ref[...]).astype(BF16)
        z_ref[...] = (_dot(h, wc_ref[...]) * _dot(h, wx_ref[...])).astype(BF16)

    @pl.when(pl.program_id(1) == 0)
    def _():
        h = _rms(x_ref[...], g_ref[...]).astype(BF16)
        h_ref[...] = h
        project(h)

    @pl.when(pl.program_id(1) > 0)
    def _():
        project(h_ref[...])


def _conv_in(x, gains, w_in, layer, j, cast_sources=()):
    t, d = x.shape
    c = w_in.shape[2] // 3
    tm, tn = TOKEN_TILE, CONV_IN_CHUNK
    nn = c // tn
    jobs = _CastJobs(cast_sources, (t // tm) * nn, lambda i, n: i * nn + n)
    outs = pl.pallas_call(
        functools.partial(_conv_in_body, n_cast=len(jobs)),
        grid=(t // tm, nn),
        in_specs=[
            pl.BlockSpec((tm, d), lambda i, n: (i, 0)),
            pl.BlockSpec((None, 1, d), lambda i, n: (layer, 0, 0)),
            pl.BlockSpec((None, d, tn), lambda i, n: (j, 0, n)),
            pl.BlockSpec((None, d, tn), lambda i, n: (j, 0, n + nn)),
            pl.BlockSpec((None, d, tn), lambda i, n: (j, 0, n + 2 * nn)),
        ] + jobs.in_specs,
        out_specs=[
            pl.BlockSpec((tm, tn), lambda i, n: (i, n)),
            pl.BlockSpec((tm, tn), lambda i, n: (i, n)),
        ] + jobs.out_specs,
        out_shape=[jax.ShapeDtypeStruct((t, c), BF16),
                   jax.ShapeDtypeStruct((t, c), BF16)] + jobs.out_shape,
        scratch_shapes=[pltpu.VMEM((tm, d), BF16)],
        compiler_params=_params(
            ("arbitrary", "arbitrary"),
            2 * _nbytes((tm, d), F32), _nbytes((tm, d), BF16),
            6 * _nbytes((d, tn), BF16), 4 * _nbytes((tm, tn), BF16),
            3 * _nbytes((tm, tn), F32), *jobs.nbytes),
        name="conv_in",
    )(x, gains, w_in, w_in, w_in, *jobs.operands)
    return outs[0], outs[1], tuple(outs[2:])


def _conv_out_body(x_ref, z_ref, zp_ref, zn_ref, gb_ref, wc_ref, bc_ref, wo_ref, o_ref,
                   *, tiles_per_seq):
    tm = z_ref.shape[0]
    i = pl.program_id(0) % tiles_per_seq
    z = z_ref[...].astype(F32)
    halo = zp_ref.shape[0]
    before = jnp.where(i == 0, 0.0, zp_ref[...].astype(F32)[halo - 1:halo])
    after = jnp.where(i == tiles_per_seq - 1, 0.0, zn_ref[...].astype(F32)[0:1])
    row = lax.broadcasted_iota(jnp.int32, z.shape, 0)
    z_prev = jnp.where(row == 0, before, pltpu.roll(z, 1, axis=0))
    z_next = jnp.where(row == tm - 1, after, pltpu.roll(z, tm - 1, axis=0))
    conv = z_prev * wc_ref[0:1] + z * wc_ref[1:2] + z_next * wc_ref[2:3] + bc_ref[...]
    y = (gb_ref[...].astype(F32) * conv).astype(BF16)
    o_ref[...] = x_ref[...] + _dot(y, wo_ref[...])


def _conv_out(x, z, gb, w_conv, b_conv, w_out, j, seq):
    t, d = x.shape
    c = z.shape[1]
    tm = TOKEN_TILE
    halo = 16
    assert seq % tm == 0 and tm % halo == 0
    per = tm // halo
    last = t // halo - 1
    return pl.pallas_call(
        functools.partial(_conv_out_body, tiles_per_seq=seq // tm),
        grid=(t // tm,),
        in_specs=[
            pl.BlockSpec((tm, d), lambda i: (i, 0)),
            pl.BlockSpec((tm, c), lambda i: (i, 0)),
            pl.BlockSpec((halo, c), lambda i: (jnp.maximum(i * per - 1, 0), 0)),
            pl.BlockSpec((halo, c), lambda i: (jnp.minimum((i + 1) * per, last), 0)),
            pl.BlockSpec((tm, c), lambda i: (i, 0)),
            pl.BlockSpec((None,) + w_conv.shape[1:], lambda i: (j, 0, 0)),
            pl.BlockSpec((None, 1, c), lambda i: (j, 0, 0)),
            _const_spec((None, c, d), lambda i: (j, 0, 0)),
        ],
        out_specs=pl.BlockSpec((tm, d), lambda i: (i, 0)),
        out_shape=jax.ShapeDtypeStruct((t, d), F32),
        compiler_params=_params(
            ("parallel",),
            4 * _nbytes((tm, d), F32), 4 * _nbytes((tm, c), BF16),
            _nbytes((c, d), BF16), 5 * _nbytes((tm, c), F32)),
        name="conv_out",
    )(x, z, z, z, gb, w_conv, b_conv, w_out)


FFN_NAMES = ("ffn1", "ffn2")


def _trunk(x, p, w, tabs, ffn_w):
    bsz, seq, d = x.shape
    depth = p.shape[0]
    t = bsz * seq
    x = x.reshape(t, d)
    p = p.reshape(depth, t, p.shape[-1])
    tabs_even, tabs_odd = _sequence_dft_tables(seq, 0), _sequence_dft_tables(seq, 1)
    head_dim = tabs["head_dim"]

    def missing(keys):
        return [k for k in keys if k[2] < depth and k not in ffn_w]

    def sources(keys):
        return [(w[f"w_{name}_{part}"], layer) for name, part, layer in keys]

    def ffn(x, name, i):
        return _ffn(x, w[f"g_{name}"], ffn_w[(name, "in", i)], ffn_w[(name, "out", i)], i)

    for i in range(depth):
        x = ffn(x, "ffn1", i)
        j = i // 2
        keys = missing([(n, part, i + 1) for n in FFN_NAMES for part in ("in", "out")])
        if i % 2 == 0:
            pe, po, u, vn, cast = _mix_in(x, w["g_mix"], w["w_in_ab"], tabs["cs"], tabs["hm"],
                                          w["g_v"], i, j, sources(keys))
            ffn_w.update(zip(keys, cast))
            ya = _seq_dft(pe.reshape(bsz, seq // 2, pe.shape[-1]),
                          po.reshape(bsz, seq // 2, po.shape[-1]), tabs_even, tabs_odd)
            x = _mix_out(x, ya.reshape(t, ya.shape[-1]), u, vn, w["w_s"], w["bias_s"],
                         w["w_out_ab"], j, head_dim)
        else:
            gb, z, cast = _conv_in(x, w["g_mix"], w["w_in_c"], i, j, sources(keys))
            ffn_w.update(zip(keys, cast))
            x = _conv_out(x, z, gb, w["w_conv"], w["b_conv"], w["w_out_c"], j, seq)
        x = ffn(x, "ffn2", i)
        x = _ple(x, p, w["g_ple"], w["w_ple_gate"], w["w_ple"], w["g_final"], i,
                 final=(i == depth - 1))
    return x.reshape(bsz, seq, d)


def kernel(x_prompt, x_sample, p_prompt, p_sample, g_ffn1, w_ffn1_in, w_ffn1_out, g_mix, w_in_ab, g_v, w_s, b_s, w_out_ab, w_in_c, w_conv, b_conv, w_out_c, g_ffn2, w_ffn2_in, w_ffn2_out, g_ple, w_ple_gate, w_ple, g_final):
    n_even, n_heads, head_dim = g_v.shape
    chunk = w_s.shape[-1]
    gain = lambda g: g[:, None, :]
    w = {
        "g_ffn1": gain(g_ffn1), "g_mix": gain(g_mix), "g_ffn2": gain(g_ffn2),
        "g_ple": gain(g_ple), "g_final": g_final[None, :],
        "g_v": g_v.reshape(n_even, 1, n_heads * head_dim),
        "w_ffn1_in": w_ffn1_in, "w_ffn1_out": w_ffn1_out,
        "w_ffn2_in": w_ffn2_in, "w_ffn2_out": w_ffn2_out,
        "w_in_ab": w_in_ab.astype(BF16), "w_out_ab": w_out_ab.astype(BF16),
        "w_s": w_s.astype(BF16).reshape(n_even, n_heads * chunk, chunk),
        "bias_s": jnp.repeat(jnp.swapaxes(b_s, 1, 2), head_dim, axis=2),
        "w_in_c": w_in_c.astype(BF16), "w_out_c": w_out_c.astype(BF16),
        "w_conv": w_conv, "b_conv": b_conv[:, None, :],
        "w_ple_gate": w_ple_gate.astype(BF16), "w_ple": w_ple.astype(BF16),
    }
    tabs = {"cs": _channel_dft_table(), "hm": _head_mean_table(head_dim), "head_dim": head_dim}
    ffn_w = {(name, part, 0): _cast_layer(w[f"w_{name}_{part}"], 0)
             for name in FFN_NAMES for part in ("in", "out")}
    y_prompt = _trunk(x_prompt, p_prompt, w, tabs, ffn_w)
    y_sample = _trunk(x_sample, p_sample, w, tabs, ffn_w)
    return (y_prompt, y_sample)
```

```python
import functools
import math

import numpy as np
import jax
import jax.numpy as jnp
from jax import lax
from jax.experimental import pallas as pl
from jax.experimental.pallas import tpu as pltpu

EPS = 1e-6
A_HEAD_DIM = 64
BF16 = jnp.bfloat16
F32 = jnp.float32

V7X_VMEM_BYTES = 64 * 1024 * 1024
V7X_MXU_DIM = 256
V7X_LANES = 128
BF16_SUBLANES = 16
VMEM_RESERVE_BYTES = 6 * 1024 * 1024

TOKEN_TILE = 1024
MIX_IN_TOKEN_TILE = 512
FF_CHUNK = 512
CONV_IN_CHUNK = 512
DFT_ROW_TILE = 1024
DFT_K_CHUNK = 1024
CAST_BLOCK_BYTES = 4 * 1024 * 1024


def _vmem_limit(*nbytes):
    return int(min(sum(nbytes) + VMEM_RESERVE_BYTES, V7X_VMEM_BYTES - 2 * 1024 * 1024))


def _nbytes(shape, dtype):
    return int(np.prod(shape)) * jnp.dtype(dtype).itemsize


def _params(semantics, *nbytes):
    return pltpu.CompilerParams(dimension_semantics=semantics,
                                vmem_limit_bytes=_vmem_limit(*nbytes))


def _const_spec(block, index_map):
    return pl.BlockSpec(block, index_map, pipeline_mode=pl.Buffered(1))


def _rms(x, g):
    ms = jnp.mean(x * x, axis=-1, keepdims=True)
    return x * lax.rsqrt(ms + EPS) * g


def _dot(a, b):
    return jnp.dot(a, b, preferred_element_type=F32)


def _ffn_body(x_ref, g_ref, wg_ref, wu_ref, wo_ref, o_ref, h_ref):
    j = pl.program_id(1)

    def half_swiglu(h):
        gate = _dot(h, wg_ref[...])
        up = _dot(h, wu_ref[...])
        act = (gate * jax.nn.sigmoid(gate) * up).astype(BF16)
        return 0.5 * _dot(act, wo_ref[...])

    @pl.when(j == 0)
    def _():
        x = x_ref[...]
        h = _rms(x, g_ref[...]).astype(BF16)
        h_ref[...] = h
        o_ref[...] = x + half_swiglu(h)

    @pl.when(j > 0)
    def _():
        o_ref[...] += half_swiglu(h_ref[...])


def _cast_rows(rows, steps):
    per = -(-rows // steps)
    per = -(-per // BF16_SUBLANES) * BF16_SUBLANES
    while per < rows and rows % per:
        per += BF16_SUBLANES
    return per if rows % per == 0 else None


class _CastJobs:
    def __init__(self, sources, steps, step_of):
        self.in_specs, self.out_specs, self.out_shape = [], [], []
        self.operands, self.nbytes = [], []
        for src, layer in sources:
            rows, cols = src.shape[1:]
            per = _cast_rows(rows, steps)
            assert per is not None
            last = rows // per - 1
            block = lambda *idx, last=last: jnp.minimum(step_of(*idx), last)
            self.in_specs.append(pl.BlockSpec(
                (None, per, cols), lambda *idx, block=block, layer=layer: (layer, block(*idx), 0)))
            self.out_specs.append(pl.BlockSpec(
                (per, cols), lambda *idx, block=block: (block(*idx), 0)))
            self.out_shape.append(jax.ShapeDtypeStruct((rows, cols), BF16))
            self.operands.append(src)
            self.nbytes += [2 * _nbytes((per, cols), F32), 2 * _nbytes((per, cols), BF16)]

    def __len__(self):
        return len(self.operands)


def _cast_blocks(cast_in, cast_out):
    for src, dst in zip(cast_in, cast_out):
        dst[...] = src[...].astype(BF16)


def _cast_body(src_ref, dst_ref):
    dst_ref[...] = src_ref[...].astype(BF16)


def _cast_layer(src, layer):
    rows, cols = src.shape[1:]
    per = _cast_rows(rows, max(1, _nbytes((rows, cols), F32) // CAST_BLOCK_BYTES))
    assert per is not None
    return pl.pallas_call(
        _cast_body,
        grid=(rows // per,),
        in_specs=[pl.BlockSpec((None, per, cols), lambda i: (layer, i, 0))],
        out_specs=pl.BlockSpec((per, cols), lambda i: (i, 0)),
        out_shape=jax.ShapeDtypeStruct((rows, cols), BF16),
        compiler_params=_params(
            ("parallel",), 2 * _nbytes((per, cols), F32), 2 * _nbytes((per, cols), BF16)),
        name="cast",
    )(src)


def _ffn(x, gains, w_in, w_out, layer):
    t, d = x.shape
    f = w_out.shape[0]
    tm, tk = TOKEN_TILE, FF_CHUNK
    nk = f // tk
    assert t % tm == 0 and f % tk == 0
    return pl.pallas_call(
        _ffn_body,
        grid=(t // tm, nk),
        in_specs=[
            pl.BlockSpec((tm, d), lambda i, j: (i, 0)),
            pl.BlockSpec((None, 1, d), lambda i, j: (layer, 0, 0)),
            pl.BlockSpec((d, tk), lambda i, j: (0, j)),
            pl.BlockSpec((d, tk), lambda i, j: (0, j + nk)),
            pl.BlockSpec((tk, d), lambda i, j: (j, 0)),
        ],
        out_specs=pl.BlockSpec((tm, d), lambda i, j: (i, 0)),
        out_shape=jax.ShapeDtypeStruct((t, d), F32),
        scratch_shapes=[pltpu.VMEM((tm, d), BF16)],
        compiler_params=_params(
            ("parallel", "arbitrary"),
            4 * _nbytes((tm, d), F32), _nbytes((tm, d), BF16),
            6 * _nbytes((d, tk), BF16), 3 * _nbytes((tm, tk), F32)),
        name="ffn",
    )(x, gains, w_in, w_in, w_out)


def _ple_body(x_ref, p_ref, g_ref, wgate_ref, wp_ref, gf_ref, o_ref, *, final):
    x = x_ref[...]
    h = _rms(x, g_ref[...]).astype(BF16)
    gate = jax.nn.sigmoid(_dot(h, wgate_ref[...]))
    emb = _dot(p_ref[...].astype(BF16), wp_ref[...])
    y = x + gate * emb
    if final:
        y = _rms(y, gf_ref[...])
    o_ref[...] = y


def _ple(x, p, gains, w_gate, w_p, g_final, layer, final):
    t, d = x.shape
    e = p.shape[-1]
    tm = TOKEN_TILE
    return pl.pallas_call(
        functools.partial(_ple_body, final=final),
        grid=(t // tm,),
        in_specs=[
            pl.BlockSpec((tm, d), lambda i: (i, 0)),
            pl.BlockSpec((None, tm, e), lambda i: (layer, i, 0)),
            pl.BlockSpec((None, 1, d), lambda i: (layer, 0, 0)),
            _const_spec((None, d, d), lambda i: (layer, 0, 0)),
            _const_spec((None, e, d), lambda i: (layer, 0, 0)),
            pl.BlockSpec((1, d), lambda i: (0, 0)),
        ],
        out_specs=pl.BlockSpec((tm, d), lambda i: (i, 0)),
        out_shape=jax.ShapeDtypeStruct((t, d), F32),
        compiler_params=_params(
            ("parallel",),
            4 * _nbytes((tm, d), F32), 2 * _nbytes((tm, e), F32),
            _nbytes((d, d), BF16), _nbytes((e, d), BF16), 3 * _nbytes((tm, d), F32)),
        name="ple",
    )(x, p, gains, w_gate, w_p, g_final)


def _channel_dft_table():
    n = A_HEAD_DIM
    heads = V7X_MXU_DIM // n
    idx = np.arange(n)
    ang = 2.0 * np.pi * ((idx[:, None] * idx[None, :]) % n) / n
    eye = np.eye(heads)
    cos = np.kron(eye, np.cos(ang)) / math.sqrt(n)
    sin = np.kron(eye, np.sin(ang)) / math.sqrt(n)
    return jnp.asarray(np.concatenate([cos, sin], axis=1), dtype=BF16)


def _head_mean_table(head_dim):
    heads = V7X_MXU_DIM // head_dim
    m = np.kron(np.eye(heads), np.full((head_dim, head_dim), 1.0 / head_dim))
    return jnp.asarray(m, dtype=BF16)


def _sequence_dft_tables(s, parity):
    base = V7X_MXU_DIM
    half = s // 2
    pos = (2 * jnp.arange(half, dtype=jnp.int32) + parity)[None, :]
    lo = jnp.arange(base, dtype=jnp.int32)[:, None]
    hi = jnp.arange(half // base, dtype=jnp.int32)[:, None] * base
    step = F32(2.0 * math.pi / s)
    ang_lo = ((lo * pos) % s).astype(F32) * step
    ang_hi = ((hi * pos) % s).astype(F32) * step
    scale = F32(1.0 / math.sqrt(s))
    c_lo, s_lo = jnp.cos(ang_lo)[None], jnp.sin(ang_lo)[None]
    c_hi, s_hi = (jnp.cos(ang_hi) * scale)[:, None], (jnp.sin(ang_hi) * scale)[:, None]
    return c_lo[0], s_lo[0], jnp.stack([c_hi[:, 0], s_hi[:, 0], -s_hi[:, 0]])


def _mix_in_body(x_ref, g_ref, wa_ref, wu_ref, wv_ref, cs_ref, hm_ref, gv_ref, *rest, n_cast):
    cast_in, rest = rest[:n_cast], rest[n_cast:]
    pe_ref, po_ref, u_ref, vn_ref = rest[:4]
    cast_out, p_ref = rest[4:4 + n_cast], rest[4 + n_cast]
    _cast_blocks(cast_in, cast_out)
    a = wa_ref.shape[1]
    w = V7X_MXU_DIM
    half = pe_ref.shape[0]
    h = _rms(x_ref[...], g_ref[...]).astype(BF16)
    za = _dot(h, wa_ref[...]).astype(BF16)
    lanes = p_ref.shape[-1]
    per_tile = w // lanes
    for t in range(a // w):
        pcs = _dot(za[:, t * w:(t + 1) * w], cs_ref[...])
        for c in range(2 * per_tile):
            n = (c // per_tile) * (a // lanes) + t * per_tile + c % per_tile
            p_ref[n] = pcs[:, c * lanes:(c + 1) * lanes]
    for n in range(p_ref.shape[0]):
        cols = slice(n * lanes, (n + 1) * lanes)
        pe_ref[:, cols] = p_ref[n, pl.ds(0, half, stride=2), :].astype(BF16)
        po_ref[:, cols] = p_ref[n, pl.ds(1, half, stride=2), :].astype(BF16)
    u_ref[...] = jax.nn.gelu(_dot(h, wu_ref[...]))
    v = jax.nn.gelu(_dot(h, wv_ref[...]))
    sq = (v * v).astype(BF16)
    for t in range(v.shape[1] // w):
        cols = slice(t * w, (t + 1) * w)
        ms = _dot(sq[:, cols], hm_ref[...])
        vn_ref[:, cols] = (v[:, cols] * lax.rsqrt(ms + EPS) * gv_ref[:, cols]).astype(BF16)


def _mix_in(x, gains, w_in, cs_tab, hm_tab, g_v, layer, j, cast_sources=()):
    t, d = x.shape
    a = d // 2
    b = d - a
    tm = MIX_IN_TOKEN_TILE
    assert w_in.shape[2] == a + 2 * b and a == b
    jobs = _CastJobs(cast_sources, t // tm, lambda i: i)
    outs = pl.pallas_call(
        functools.partial(_mix_in_body, n_cast=len(jobs)),
        grid=(t // tm,),
        in_specs=[
            pl.BlockSpec((tm, d), lambda i: (i, 0)),
            pl.BlockSpec((None, 1, d), lambda i: (layer, 0, 0)),
            _const_spec((None, d, a), lambda i: (j, 0, 0)),
            _const_spec((None, d, b), lambda i: (j, 0, 1)),
            _const_spec((None, d, b), lambda i: (j, 0, 2)),
            _const_spec(cs_tab.shape, lambda i: (0, 0)),
            _const_spec(hm_tab.shape, lambda i: (0, 0)),
            pl.BlockSpec((None, 1, b), lambda i: (j, 0, 0)),
        ] + jobs.in_specs,
        out_specs=[
            pl.BlockSpec((tm // 2, 2 * a), lambda i: (i, 0)),
            pl.BlockSpec((tm // 2, 2 * a), lambda i: (i, 0)),
            pl.BlockSpec((tm, b), lambda i: (i, 0)),
            pl.BlockSpec((tm, b), lambda i: (i, 0)),
        ] + jobs.out_specs,
        out_shape=[
            jax.ShapeDtypeStruct((t // 2, 2 * a), BF16),
            jax.ShapeDtypeStruct((t // 2, 2 * a), BF16),
            jax.ShapeDtypeStruct((t, b), F32),
            jax.ShapeDtypeStruct((t, b), BF16),
        ] + jobs.out_shape,
        scratch_shapes=[pltpu.VMEM((2 * a // V7X_LANES, tm, V7X_LANES), F32)],
        compiler_params=_params(
            ("arbitrary",),
            2 * _nbytes((tm, d), F32), 3 * _nbytes((d, a), BF16),
            2 * _nbytes((tm, 2 * a), BF16), 2 * _nbytes((tm, b), F32),
            2 * _nbytes((tm, b), BF16), _nbytes((tm, 2 * a), F32),
            4 * _nbytes((tm, a), F32), *jobs.nbytes),
        name="mix_in",
    )(x, gains, w_in, w_in, w_in, cs_tab, hm_tab, g_v, *jobs.operands)
    return outs[0], outs[1], outs[2], outs[3], tuple(outs[4:])


def _dft_tile(c_lo_ref, s_lo_ref, hi_ref):
    c_lo, s_lo = c_lo_ref[...], s_lo_ref[...]
    cos, nsin = [], []
    for q in range(hi_ref.shape[1]):
        c_hi, s_hi, ns_hi = (hi_ref[n, q:q + 1, :] for n in range(3))
        cos.append((c_hi * c_lo - s_hi * s_lo).astype(BF16))
        nsin.append((ns_hi * c_lo - c_hi * s_lo).astype(BF16))
    return jnp.concatenate(cos, axis=0), jnp.concatenate(nsin, axis=0)


def _seq_dft_body(cle_ref, sle_ref, hie_ref, clo_ref, slo_ref, hio_ref, pe_ref, po_ref, o_ref,
                  *acc, k_steps):
    a = o_ref.shape[-1]
    kk = pl.program_id(2)
    if k_steps > 1:
        acc_e, acc_o = acc

        @pl.when(kk == 0)
        def _():
            acc_e[...] = jnp.zeros_like(acc_e)
            acc_o[...] = jnp.zeros_like(acc_o)

    ce, se = _dft_tile(cle_ref, sle_ref, hie_ref)
    co, so = _dft_tile(clo_ref, slo_ref, hio_ref)
    even = _dot(ce, pe_ref[:, :a]) + _dot(se, pe_ref[:, a:])
    odd = _dot(co, po_ref[:, :a]) + _dot(so, po_ref[:, a:])
    if k_steps == 1:
        o_ref[0] = (even + odd).astype(BF16)
        o_ref[1] = (even - odd).astype(BF16)
        return
    acc_e[...] += even
    acc_o[...] += odd

    @pl.when(kk == k_steps - 1)
    def _():
        o_ref[0] = (acc_e[...] + acc_o[...]).astype(BF16)
        o_ref[1] = (acc_e[...] - acc_o[...]).astype(BF16)


def _seq_dft(pe, po, tabs_even, tabs_odd):
    bsz, half, a2 = pe.shape
    a = a2 // 2
    tr = min(DFT_ROW_TILE, half)
    tc = min(DFT_K_CHUNK, half)
    k_steps = half // tc
    base = tabs_even[0].shape[0]
    assert tr % base == 0
    n_hi = tr // base
    tabs_even, tabs_odd = [(c, s, hi.reshape(3, half // tr, n_hi, half))
                           for c, s, hi in (tabs_even, tabs_odd)]
    lo_spec = pl.BlockSpec((base, tc), lambda b, r, k: (0, k))
    hi_spec = pl.BlockSpec((3, None, n_hi, tc), lambda b, r, k: (0, r, 0, k))
    p_spec = pl.BlockSpec((None, tc, a2), lambda b, r, k: (b, k, 0))
    return pl.pallas_call(
        functools.partial(_seq_dft_body, k_steps=k_steps),
        grid=(bsz, half // tr, k_steps),
        in_specs=[lo_spec, lo_spec, hi_spec, lo_spec, lo_spec, hi_spec, p_spec, p_spec],
        out_specs=pl.BlockSpec((None, 2, tr, a), lambda b, r, k: (b, 0, r, 0)),
        out_shape=jax.ShapeDtypeStruct((bsz, 2, half, a), BF16),
        scratch_shapes=[pltpu.VMEM((tr, a), F32)] * 2 if k_steps > 1 else [],
        compiler_params=_params(
            ("parallel", "parallel", "arbitrary"),
            8 * _nbytes((tr, tc), BF16), 4 * _nbytes((tc, a2), BF16),
            2 * _nbytes((2, tr, a), BF16), 4 * _nbytes((tr, a), F32)),
        name="seq_dft",
    )(*tabs_even, *tabs_odd, pe, po)


def _mix_out_body(x_ref, ya_ref, u_ref, vn_ref, ws_ref, bias_ref, wo_ref, o_ref, y_ref,
                  *, head_dim):
    tm, a = ya_ref.shape
    chunk = ws_ref.shape[1]
    w = V7X_MXU_DIM
    heads = w // head_dim
    lane_head = lax.broadcasted_iota(jnp.int32, (chunk, w), 1) // head_dim
    y_ref[:, :a] = ya_ref[...]
    for c in range(tm // chunk):
        rows = slice(c * chunk, (c + 1) * chunk)
        for q in range(vn_ref.shape[1] // w):
            cols = slice(q * w, (q + 1) * w)
            stacked = _dot(ws_ref[q * heads * chunk:(q + 1) * heads * chunk, :],
                           vn_ref[rows, cols])
            mixed = stacked[:chunk]
            for r in range(1, heads):
                mixed = jnp.where(lane_head == r, stacked[r * chunk:(r + 1) * chunk], mixed)
            yb = u_ref[rows, cols] * (mixed + bias_ref[:, cols])
            y_ref[rows, a + q * w:a + (q + 1) * w] = yb.astype(BF16)
    o_ref[...] = x_ref[...] + _dot(y_ref[...], wo_ref[...])


def _mix_out(x, ya, u, vn, w_s, bias, w_out, j, head_dim):
    t, d = x.shape
    a = ya.shape[1]
    b = u.shape[1]
    chunk = w_s.shape[2]
    tm = TOKEN_TILE
    assert tm % chunk == 0 and b % V7X_MXU_DIM == 0 and V7X_MXU_DIM % head_dim == 0
    return pl.pallas_call(
        functools.partial(_mix_out_body, head_dim=head_dim),
        grid=(t // tm,),
        in_specs=[
            pl.BlockSpec((tm, d), lambda i: (i, 0)),
            pl.BlockSpec((tm, a), lambda i: (i, 0)),
            pl.BlockSpec((tm, b), lambda i: (i, 0)),
            pl.BlockSpec((tm, b), lambda i: (i, 0)),
            _const_spec((None,) + w_s.shape[1:], lambda i: (j, 0, 0)),
            _const_spec((None,) + bias.shape[1:], lambda i: (j, 0, 0)),
            _const_spec((None, a + b, d), lambda i: (j, 0, 0)),
        ],
        out_specs=pl.BlockSpec((tm, d), lambda i: (i, 0)),
        out_shape=jax.ShapeDtypeStruct((t, d), F32),
        scratch_shapes=[pltpu.VMEM((tm, a + b), BF16)],
        compiler_params=_params(
            ("parallel",),
            4 * _nbytes((tm, d), F32), 2 * _nbytes((tm, a), BF16),
            2 * _nbytes((tm, b), F32), 2 * _nbytes((tm, b), BF16),
            _nbytes(w_s.shape[1:], BF16), _nbytes(bias.shape[1:], F32),
            _nbytes((a + b, d), BF16), _nbytes((tm, a + b), BF16),
            2 * _nbytes((tm, d), F32)),
        name="mix_out",
    )(x, ya, u, vn, w_s, bias, w_out)


def _conv_in_body(x_ref, g_ref, wb_ref, wc_ref, wx_ref, *rest, n_cast):
    cast_in, rest = rest[:n_cast], rest[n_cast:]
    gb_ref, z_ref, cast_out, h_ref = rest[0], rest[1], rest[2:2 + n_cast], rest[2 + n_cast]

    def project(h):
        _cast_blocks(cast_in, cast_out)
        gb_ref[...] = _dot(h, wb_ref[...]).astype(BF16)
        z_ref[...] = (_dot(h, wc_ref[...]) * _dot(h, wx_ref[...])).astype(BF16)

    @pl.when(pl.program_id(1) == 0)
    def _():
        h = _rms(x_ref[...], g_ref[...]).astype(BF16)
        h_ref[...] = h
        project(h)

    @pl.when(pl.program_id(1) > 0)
    def _():
        project(h_ref[...])


def _conv_in(x, gains, w_in, layer, j, cast_sources=()):
    t, d = x.shape
    c = w_in.shape[2] // 3
    tm, tn = TOKEN_TILE, CONV_IN_CHUNK
    nn = c // tn
    jobs = _CastJobs(cast_sources, (t // tm) * nn, lambda i, n: i * nn + n)
    outs = pl.pallas_call(
        functools.partial(_conv_in_body, n_cast=len(jobs)),
        grid=(t // tm, nn),
        in_specs=[
            pl.BlockSpec((tm, d), lambda i, n: (i, 0)),
            pl.BlockSpec((None, 1, d), lambda i, n: (layer, 0, 0)),
            pl.BlockSpec((None, d, tn), lambda i, n: (j, 0, n)),
            pl.BlockSpec((None, d, tn), lambda i, n: (j, 0, n + nn)),
            pl.BlockSpec((None, d, tn), lambda i, n: (j, 0, n + 2 * nn)),
        ] + jobs.in_specs,
        out_specs=[
            pl.BlockSpec((tm, tn), lambda i, n: (i, n)),
            pl.BlockSpec((tm, tn), lambda i, n: (i, n)),
        ] + jobs.out_specs,
        out_shape=[jax.ShapeDtypeStruct((t, c), BF16),
                   jax.ShapeDtypeStruct((t, c), BF16)] + jobs.out_shape,
        scratch_shapes=[pltpu.VMEM((tm, d), BF16)],
        compiler_params=_params(
            ("arbitrary", "arbitrary"),
            2 * _nbytes((tm, d), F32), _nbytes((tm, d), BF16),
            6 * _nbytes((d, tn), BF16), 4 * _nbytes((tm, tn), BF16),
            3 * _nbytes((tm, tn), F32), *jobs.nbytes),
        name="conv_in",
    )(x, gains, w_in, w_in, w_in, *jobs.operands)
    return outs[0], outs[1], tuple(outs[2:])


def _conv_out_body(x_ref, z_ref, zp_ref, zn_ref, gb_ref, wc_ref, bc_ref, wo_ref, o_ref,
                   *, tiles_per_seq):
    tm = z_ref.shape[0]
    i = pl.program_id(0) % tiles_per_seq
    z = z_ref[...].astype(F32)
    halo = zp_ref.shape[0]
    before = jnp.where(i == 0, 0.0, zp_ref[...].astype(F32)[halo - 1:halo])
    after = jnp.where(i == tiles_per_seq - 1, 0.0, zn_ref[...].astype(F32)[0:1])
    row = lax.broadcasted_iota(jnp.int32, z.shape, 0)
    z_prev = jnp.where(row == 0, before, pltpu.roll(z, 1, axis=0))
    z_next = jnp.where(row == tm - 1, after, pltpu.roll(z, tm - 1, axis=0))
    conv = z_prev * wc_ref[0:1] + z * wc_ref[1:2] + z_next * wc_ref[2:3] + bc_ref[...]
    y = (gb_ref[...].astype(F32) * conv).astype(BF16)
    o_ref[...] = x_ref[...] + _dot(y, wo_ref[...])


def _conv_out(x, z, gb, w_conv, b_conv, w_out, j, seq):
    t, d = x.shape
    c = z.shape[1]
    tm = TOKEN_TILE
    halo = 16
    assert seq % tm == 0 and tm % halo == 0
    per = tm // halo
    last = t // halo - 1
    return pl.pallas_call(
        functools.partial(_conv_out_body, tiles_per_seq=seq // tm),
        grid=(t // tm,),
        in_specs=[
            pl.BlockSpec((tm, d), lambda i: (i, 0)),
            pl.BlockSpec((tm, c), lambda i: (i, 0)),
            pl.BlockSpec((halo, c), lambda i: (jnp.maximum(i * per - 1, 0), 0)),
            pl.BlockSpec((halo, c), lambda i: (jnp.minimum((i + 1) * per, last), 0)),
            pl.BlockSpec((tm, c), lambda i: (i, 0)),
            pl.BlockSpec((None,) + w_conv.shape[1:], lambda i: (j, 0, 0)),
            pl.BlockSpec((None, 1, c), lambda i: (j, 0, 0)),
            _const_spec((None, c, d), lambda i: (j, 0, 0)),
        ],
        out_specs=pl.BlockSpec((tm, d), lambda i: (i, 0)),
        out_shape=jax.ShapeDtypeStruct((t, d), F32),
        compiler_params=_params(
            ("parallel",),
            4 * _nbytes((tm, d), F32), 4 * _nbytes((tm, c), BF16),
            _nbytes((c, d), BF16), 5 * _nbytes((tm, c), F32)),
        name="conv_out",
    )(x, z, z, z, gb, w_conv, b_conv, w_out)


FFN_NAMES = ("ffn1", "ffn2")


def _trunk(x, p, w, tabs, ffn_w):
    bsz, seq, d = x.shape
    depth = p.shape[0]
    t = bsz * seq
    x = x.reshape(t, d)
    p = p.reshape(depth, t, p.shape[-1])
    tabs_even, tabs_odd = _sequence_dft_tables(seq, 0), _sequence_dft_tables(seq, 1)
    head_dim = tabs["head_dim"]

    def missing(keys):
        return [k for k in keys if k[2] < depth and k not in ffn_w]

    def sources(keys):
        return [(w[f"w_{name}_{part}"], layer) for name, part, layer in keys]

    def ffn(x, name, i):
        return _ffn(x, w[f"g_{name}"], ffn_w[(name, "in", i)], ffn_w[(name, "out", i)], i)

    for i in range(depth):
        x = ffn(x, "ffn1", i)
        j = i // 2
        keys = missing([(n, part, i + 1) for n in FFN_NAMES for part in ("in", "out")])
        if i % 2 == 0:
            pe, po, u, vn, cast = _mix_in(x, w["g_mix"], w["w_in_ab"], tabs["cs"], tabs["hm"],
                                          w["g_v"], i, j, sources(keys))
            ffn_w.update(zip(keys, cast))
            ya = _seq_dft(pe.reshape(bsz, seq // 2, pe.shape[-1]),
                          po.reshape(bsz, seq // 2, po.shape[-1]), tabs_even, tabs_odd)
            x = _mix_out(x, ya.reshape(t, ya.shape[-1]), u, vn, w["w_s"], w["bias_s"],
                         w["w_out_ab"], j, head_dim)
        else:
            gb, z, cast = _conv_in(x, w["g_mix"], w["w_in_c"], i, j, sources(keys))
            ffn_w.update(zip(keys, cast))
            x = _conv_out(x, z, gb, w["w_conv"], w["b_conv"], w["w_out_c"], j, seq)
        x = ffn(x, "ffn2", i)
        x = _ple(x, p, w["g_ple"], w["w_ple_gate"], w["w_ple"], w["g_final"], i,
                 final=(i == depth - 1))
    return x.reshape(bsz, seq, d)


def kernel(x_prompt, x_sample, p_prompt, p_sample, g_ffn1, w_ffn1_in, w_ffn1_out, g_mix, w_in_ab, g_v, w_s, b_s, w_out_ab, w_in_c, w_conv, b_conv, w_out_c, g_ffn2, w_ffn2_in, w_ffn2_out, g_ple, w_ple_gate, w_ple, g_final):
    n_even, n_heads, head_dim = g_v.shape
    chunk = w_s.shape[-1]
    gain = lambda g: g[:, None, :]
    w = {
        "g_ffn1": gain(g_ffn1), "g_mix": gain(g_mix), "g_ffn2": gain(g_ffn2),
        "g_ple": gain(g_ple), "g_final": g_final[None, :],
        "g_v": g_v.reshape(n_even, 1, n_heads * head_dim),
        "w_ffn1_in": w_ffn1_in, "w_ffn1_out": w_ffn1_out,
        "w_ffn2_in": w_ffn2_in, "w_ffn2_out": w_ffn2_out,
        "w_in_ab": w_in_ab.astype(BF16), "w_out_ab": w_out_ab.astype(BF16),
        "w_s": w_s.astype(BF16).reshape(n_even, n_heads * chunk, chunk),
        "bias_s": jnp.repeat(jnp.swapaxes(b_s, 1, 2), head_dim, axis=2),
        "w_in_c": w_in_c.astype(BF16), "w_out_c": w_out_c.astype(BF16),
        "w_conv": w_conv, "b_conv": b_conv[:, None, :],
        "w_ple_gate": w_ple_gate.astype(BF16), "w_ple": w_ple.astype(BF16),
    }
    tabs = {"cs": _channel_dft_table(), "hm": _head_mean_table(head_dim), "head_dim": head_dim}
    ffn_w = {(name, part, 0): _cast_layer(w[f"w_{name}_{part}"], 0)
             for name in FFN_NAMES for part in ("in", "out")}
    y_prompt = _trunk(x_prompt, p_prompt, w, tabs, ffn_w)
    y_sample = _trunk(x_sample, p_sample, w, tabs, ffn_w)
    return (y_prompt, y_sample)
```
